```python
import jax, jax.numpy as jnp
from jax import lax
import numpy as np

D_MODEL = 2048
BATCH = 8
SEQ = 4096
DEPTH = 2
DEC_BATCH = 2
DEC_SEQ = 8192
PAST_LEN = 128

GRID_W = 64
HEAD_DIM = 128
NA_HEADS = 8
NA_WIDTH = NA_HEADS * HEAD_DIM
NA_WIN_ROWS_MAX = 8
NA_WIN_COLS = 16
RPB_ROWS = 2 * NA_WIN_ROWS_MAX - 1
RPB_COLS = 2 * NA_WIN_COLS - 1
MEM_HEADS = 4
MEM_WIDTH = MEM_HEADS * HEAD_DIM
N_MEM = 256
CONV_CH = D_MODEL - NA_WIDTH - MEM_WIDTH
CONV_K = 31
D_FF = 5632
FFN_CONV_K = 3
IN_WIDTH = 3 * NA_WIDTH + MEM_WIDTH + 2 * CONV_CH
SPLITS = [NA_WIDTH, 2 * NA_WIDTH, 3 * NA_WIDTH, 3 * NA_WIDTH + MEM_WIDTH, 3 * NA_WIDTH + MEM_WIDTH + CONV_CH]
EPS = 1e-6

kernel_name = "hybrid_natten_conformer_mem_encoder"


def rms_norm(x, g):
    xf = x.astype(jnp.float32)
    y = xf * lax.rsqrt(jnp.mean(xf * xf, axis=-1, keepdims=True) + EPS)
    return (y * g.astype(jnp.float32)).astype(x.dtype)


def layer_norm(x, g, b):
    xf = x.astype(jnp.float32)
    mu = jnp.mean(xf, axis=-1, keepdims=True)
    var = jnp.mean(jnp.square(xf - mu), axis=-1, keepdims=True)
    y = (xf - mu) * lax.rsqrt(var + EPS)
    return (y * g.astype(jnp.float32) + b.astype(jnp.float32)).astype(x.dtype)


def dwconv1d(x, w, b):
    k = w.shape[0]
    y = lax.conv_general_dilated(
        x, w[:, None, :], window_strides=(1,), padding=[((k - 1) // 2, k // 2)],
        dimension_numbers=('NWC', 'WIO', 'NWC'), feature_group_count=x.shape[-1])
    return y + b


def heads(a, n):
    return a.reshape(a.shape[0], a.shape[1], n, HEAD_DIM)


def neighbourhood_attention(q, k, v, rpb):
    b, t, h, dh = q.shape
    rows = t // GRID_W
    wr = min(NA_WIN_ROWS_MAX, rows)
    to_grid = lambda a: a.reshape(b, rows, GRID_W, h, dh).transpose(1, 0, 3, 2, 4)
    qg, kg, vg = to_grid(q), to_grid(k), to_grid(v)
    qcol = np.arange(GRID_W)
    col_start = np.clip(qcol - NA_WIN_COLS // 2, 0, GRID_W - NA_WIN_COLS)
    cols = col_start[:, None] + np.arange(NA_WIN_COLS)[None, :]
    dc_idx = cols - qcol[:, None] + (NA_WIN_COLS - 1)
    scale = dh ** -0.5

    def row_block(args):
        r, q_r = args
        rs = jnp.clip(r - wr // 2, 0, rows - wr)
        k_r = lax.dynamic_slice_in_dim(kg, rs, wr, axis=0)
        v_r = lax.dynamic_slice_in_dim(vg, rs, wr, axis=0)
        k_sel = k_r[:, :, :, cols, :]
        v_sel = v_r[:, :, :, cols, :]
        s = jnp.einsum('bhqd,wbhqcd->bhqwc', q_r, k_sel,
                       preferred_element_type=jnp.float32) * scale
        dr_idx = rs + jnp.arange(wr) - r + (NA_WIN_ROWS_MAX - 1)
        bias = rpb[:, dr_idx][:, :, dc_idx]
        s = s + bias.transpose(0, 2, 1, 3).astype(jnp.float32)[None]
        p = jax.nn.softmax(s.reshape(b, h, GRID_W, wr * NA_WIN_COLS), axis=-1)
        p = p.reshape(s.shape).astype(v.dtype)
        return jnp.einsum('bhqwc,wbhqcd->bhqd', p, v_sel)

    out = lax.map(row_block, (jnp.arange(rows), qg))
    return out.transpose(1, 0, 3, 2, 4).reshape(b, t, h * dh)


def memory_attention(q, k, v):
    s = jnp.einsum('bthd,bmhd->bhtm', q, k, preferred_element_type=jnp.float32) * (HEAD_DIM ** -0.5)
    p = jax.nn.softmax(s, axis=-1).astype(v.dtype)
    o = jnp.einsum('bhtm,bmhd->bthd', p, v)
    return o.reshape(o.shape[0], o.shape[1], MEM_WIDTH)


def encoder_layer(x, mem, norm_mix_g, w_in, na_q_norm_g, na_k_norm_g, na_rpb, mem_norm_g, w_mem_kv,
                  mem_q_norm_g, mem_k_norm_g, conv_dw_w, conv_dw_b, conv_ln_g, conv_ln_b, w_out,
                  norm_ffn_g, w_up, ffn_dw_w, ffn_dw_b, w_down):
    h = rms_norm(x, norm_mix_g)
    z = h @ w_in
    q_na, k_na, v_na, q_m, u_c, g_c = jnp.split(z, SPLITS, axis=-1)
    q_na = rms_norm(heads(q_na, NA_HEADS), na_q_norm_g)
    k_na = rms_norm(heads(k_na, NA_HEADS), na_k_norm_g)
    o_na = neighbourhood_attention(q_na, k_na, heads(v_na, NA_HEADS), na_rpb)
    kv_m = rms_norm(mem, mem_norm_g) @ w_mem_kv
    k_m, v_m = jnp.split(kv_m, 2, axis=-1)
    o_m = memory_attention(rms_norm(heads(q_m, MEM_HEADS), mem_q_norm_g),
                           rms_norm(heads(k_m, MEM_HEADS), mem_k_norm_g),
                           heads(v_m, MEM_HEADS))
    c = u_c * jax.nn.sigmoid(g_c)
    c = dwconv1d(c, conv_dw_w, conv_dw_b)
    c = jax.nn.silu(layer_norm(c, conv_ln_g, conv_ln_b))
    x = x + jnp.concatenate([o_na, o_m, c], axis=-1) @ w_out
    up = rms_norm(x, norm_ffn_g) @ w_up
    up = dwconv1d(up, ffn_dw_w, ffn_dw_b)
    gate, val = jnp.split(up, 2, axis=-1)
    return x + (jax.nn.silu(gate) * val) @ w_down


def setup_inputs(seed: int = 0) -> dict:
    key = jax.random.key(seed)
    ks = jax.random.split(key, 24)
    n = lambda k, s, sc: jax.random.normal(k, s, jnp.float32) * sc
    gain = lambda k, s: 1.0 + 0.02 * jax.random.normal(k, s, jnp.float32)
    L = DEPTH
    return {
        "x_prompt": n(ks[0], (BATCH, SEQ, D_MODEL), 1.0),
        "x_sample": n(ks[1], (DEC_BATCH, DEC_SEQ, D_MODEL), 1.0),
        "mem_prompt": n(ks[2], (BATCH, N_MEM, D_MODEL), 1.0),
        "mem_sample": n(ks[3], (DEC_BATCH, N_MEM, D_MODEL), 1.0),
        "norm_mix_g": gain(ks[4], (L, D_MODEL)),
        "w_in": n(ks[5], (L, D_MODEL, IN_WIDTH), D_MODEL ** -0.5),
        "na_q_norm_g": gain(ks[6], (L, HEAD_DIM)),
        "na_k_norm_g": gain(ks[7], (L, HEAD_DIM)),
        "na_rpb": n(ks[8], (L, NA_HEADS, RPB_ROWS, RPB_COLS), 0.1),
        "mem_norm_g": gain(ks[9], (L, D_MODEL)),
        "w_mem_kv": n(ks[10], (L, D_MODEL, 2 * MEM_WIDTH), D_MODEL ** -0.5),
        "mem_q_norm_g": gain(ks[11], (L, HEAD_DIM)),
        "mem_k_norm_g": gain(ks[12], (L, HEAD_DIM)),
        "conv_dw_w": n(ks[13], (L, CONV_K, CONV_CH), CONV_K ** -0.5),
        "conv_dw_b": n(ks[14], (L, CONV_CH), 0.02),
        "conv_ln_g": gain(ks[15], (L, CONV_CH)),
        "conv_ln_b": n(ks[16], (L, CONV_CH), 0.02),
        "w_out": n(ks[17], (L, D_MODEL, D_MODEL), D_MODEL ** -0.5),
        "norm_ffn_g": gain(ks[18], (L, D_MODEL)),
        "w_up": n(ks[19], (L, D_MODEL, 2 * D_FF), D_MODEL ** -0.5),
        "ffn_dw_w": n(ks[20], (L, FFN_CONV_K, 2 * D_FF), FFN_CONV_K ** -0.5),
        "ffn_dw_b": n(ks[21], (L, 2 * D_FF), 0.02),
        "w_down": n(ks[22], (L, D_FF, D_MODEL), D_FF ** -0.5),
    }


def reference(x_prompt, x_sample, mem_prompt, mem_sample, norm_mix_g, w_in, na_q_norm_g, na_k_norm_g,
              na_rpb, mem_norm_g, w_mem_kv, mem_q_norm_g, mem_k_norm_g, conv_dw_w, conv_dw_b, conv_ln_g,
              conv_ln_b, w_out, norm_ffn_g, w_up, ffn_dw_w, ffn_dw_b, w_down):
    params = (norm_mix_g, w_in, na_q_norm_g, na_k_norm_g, na_rpb, mem_norm_g, w_mem_kv, mem_q_norm_g,
              mem_k_norm_g, conv_dw_w, conv_dw_b, conv_ln_g, conv_ln_b, w_out, norm_ffn_g, w_up,
              ffn_dw_w, ffn_dw_b, w_down)

    def trunk(x, mem):
        for l in range(DEPTH):
            x = encoder_layer(x, mem, *[p[l] for p in params])
        return x

    y_prompt = trunk(x_prompt, mem_prompt)
    y_sample = trunk(x_sample, mem_sample)
    return (y_prompt, y_sample)
```

```python
import functools

import jax
import jax.numpy as jnp
from jax import lax
from jax.experimental import pallas as pl
from jax.experimental.pallas import tpu as pltpu

F32 = jnp.float32
BF16 = jnp.bfloat16

D_MODEL = 2048
GRID_W = 64
HEAD_DIM = 128
NA_HEADS = 8
NA_WIDTH = NA_HEADS * HEAD_DIM
NA_WIN_ROWS = 8
NA_WIN_COLS = 16
RPB_ROWS = 2 * NA_WIN_ROWS - 1
RPB_COLS = 2 * NA_WIN_COLS - 1
MEM_HEADS = 4
MEM_WIDTH = MEM_HEADS * HEAD_DIM
N_MEM = 256
CONV_CH = D_MODEL - NA_WIDTH - MEM_WIDTH
CONV_K = 31
D_FF = 5632
IN_WIDTH = 3 * NA_WIDTH + MEM_WIDTH + 2 * CONV_CH
EPS = 1e-6
ATTN_SCALE = HEAD_DIM ** -0.5
MASK_VALUE = -1e30

WIN_KEYS = NA_WIN_ROWS * GRID_W
TOKEN_TILE = 512
HALO = 16
IN_CHUNK = 512
FF_BLOCK = 512
ROW_CHUNK = 32
VMEM_LIMIT = 56 * 1024 * 1024


def _rms_rows(x, g):
    ms = jnp.mean(x * x, axis=-1, keepdims=True)
    return (x * lax.rsqrt(ms + EPS)) * g


def _bias_kernel(rpb_ref, out_ref):
    l = pl.program_id(0)
    h = pl.program_id(1)
    oi = pl.program_id(2)
    shape = (GRID_W, WIN_KEYS)
    q = lax.broadcasted_iota(jnp.int32, shape, 0)
    lane = lax.broadcasted_iota(jnp.int32, shape, 1)
    kc = lane & (GRID_W - 1)
    d = kc - q + (NA_WIN_COLS - 1)
    cs = jnp.clip(q - NA_WIN_COLS // 2, 0, GRID_W - NA_WIN_COLS)
    valid = (kc >= cs) & (kc < cs + NA_WIN_COLS)
    lane_w = lax.broadcasted_iota(jnp.int32, (1, WIN_KEYS), 1) >> 6
    base = (l * NA_HEADS + h) * (RPB_ROWS * RPB_COLS)
    t = jnp.full(shape, MASK_VALUE, F32)
    for dd in range(RPB_COLS):
        row = jnp.zeros((1, WIN_KEYS), F32)
        for w in range(NA_WIN_ROWS):
            val = rpb_ref[base + (w - oi + NA_WIN_ROWS - 1) * RPB_COLS + dd]
            row = jnp.where(lane_w == w, val, row)
        t = jnp.where(valid & (d == dd), jnp.broadcast_to(row, shape), t)
    out_ref[...] = t


def _bias_tables(na_rpb):
    depth = na_rpb.shape[0]
    return pl.pallas_call(
        _bias_kernel,
        grid=(depth, NA_HEADS, NA_WIN_ROWS),
        in_specs=[pl.BlockSpec(memory_space=pltpu.SMEM)],
        out_specs=pl.BlockSpec((None, None, None, GRID_W, WIN_KEYS), lambda l, h, o: (l, h, o, 0, 0)),
        out_shape=jax.ShapeDtypeStruct((depth, NA_HEADS, NA_WIN_ROWS, GRID_W, WIN_KEYS), F32),
        name="rpb_tables",
    )(na_rpb.reshape(-1))


def _mem_kv_kernel(mem_ref, g_ref, w_ref, kg_ref, k_out, v_out):
    h = _rms_rows(mem_ref[...], g_ref[...]).astype(BF16)
    kv = jnp.dot(h, w_ref[...], preferred_element_type=F32)
    for hd in range(MEM_HEADS):
        sl = slice(hd * HEAD_DIM, (hd + 1) * HEAD_DIM)
        k_out[:, sl] = _rms_rows(kv[:, sl], kg_ref[...]).astype(BF16)
    v_out[...] = kv[:, MEM_WIDTH:].astype(BF16)


def _mem_kv(mem, layer, mem_norm_g, w_mem_kv, mem_k_norm_g):
    b = mem.shape[0]
    return pl.pallas_call(
        _mem_kv_kernel,
        grid=(b,),
        in_specs=[
            pl.BlockSpec((None, N_MEM, D_MODEL), lambda i: (i, 0, 0)),
            pl.BlockSpec((None, 1, D_MODEL), lambda i: (layer, 0, 0)),
            pl.BlockSpec((None, D_MODEL, 2 * MEM_WIDTH), lambda i: (layer, 0, 0)),
            pl.BlockSpec((None, 1, HEAD_DIM), lambda i: (layer, 0, 0)),
        ],
        out_specs=[
            pl.BlockSpec((None, N_MEM, MEM_WIDTH), lambda i: (i, 0, 0)),
            pl.BlockSpec((None, N_MEM, MEM_WIDTH), lambda i: (i, 0, 0)),
        ],
        out_shape=[jax.ShapeDtypeStruct((b, N_MEM, MEM_WIDTH), BF16)] * 2,
        compiler_params=pltpu.CompilerParams(vmem_limit_bytes=VMEM_LIMIT),
        name="mem_kv",
    )(mem, mem_norm_g, w_mem_kv, mem_k_norm_g)


def _in_proj_kernel(x_ref, ng_ref, w_ref, qg_ref, kg_ref, mqg_ref, km_ref, vm_ref,
                    q_out, k_out, v_out, om_out, glu_out, h_scr, u_scr):
    tm = x_ref.shape[0]

    def norm_chunk(c, carry):
        r0 = pl.multiple_of(c * ROW_CHUNK, ROW_CHUNK)
        h_scr[pl.ds(r0, ROW_CHUNK), :] = _rms_rows(x_ref[pl.ds(r0, ROW_CHUNK), :], ng_ref[...]).astype(BF16)
        return carry

    lax.fori_loop(0, tm // ROW_CHUNK, norm_chunk, 0)

    heads_per_chunk = IN_CHUNK // HEAD_DIM
    for c in range(IN_WIDTH // IN_CHUNK):
        z = jnp.dot(h_scr[...], w_ref[:, c * IN_CHUNK:(c + 1) * IN_CHUNK], preferred_element_type=F32)
        col = c * IN_CHUNK
        if col < 2 * NA_WIDTH:
            is_q = col < NA_WIDTH
            out, g_ref = (q_out, qg_ref) if is_q else (k_out, kg_ref)
            head0 = (col - (0 if is_q else NA_WIDTH)) // HEAD_DIM
            for j in range(heads_per_chunk):
                zh = z[:, j * HEAD_DIM:(j + 1) * HEAD_DIM]
                out[head0 + j] = _rms_rows(zh, g_ref[...]).astype(BF16)
        elif col < 3 * NA_WIDTH:
            head0 = (col - 2 * NA_WIDTH) // HEAD_DIM
            for j in range(heads_per_chunk):
                v_out[head0 + j] = z[:, j * HEAD_DIM:(j + 1) * HEAD_DIM].astype(BF16)
        elif col < 3 * NA_WIDTH + MEM_WIDTH:
            for j in range(MEM_HEADS):
                sl = slice(j * HEAD_DIM, (j + 1) * HEAD_DIM)
                qh = _rms_rows(z[:, sl], mqg_ref[...]).astype(BF16)
                s = lax.dot_general(qh, km_ref[:, sl], (((1,), (1,)), ((), ())),
                                    preferred_element_type=F32) * ATTN_SCALE
                e = jnp.exp(s - jnp.max(s, axis=-1, keepdims=True))
                o = jnp.dot(e.astype(BF16), vm_ref[:, sl], preferred_element_type=F32)
                om_out[:, sl] = (o / jnp.sum(e, axis=-1, keepdims=True)).astype(BF16)
        elif col < 3 * NA_WIDTH + MEM_WIDTH + CONV_CH:
            u_scr[...] = z
        else:
            glu_out[...] = u_scr[...] * jax.nn.sigmoid(z)


def _in_proj(x, layer, seq_len, km, vm, norm_mix_g, w_in, na_q_norm_g, na_k_norm_g, mem_q_norm_g):
    n = x.shape[0]
    tm = TOKEN_TILE
    tiles_per_seq = seq_len // tm
    vec = lambda width: pl.BlockSpec((None, 1, width), lambda i: (layer, 0, 0))
    head_major = pl.BlockSpec((NA_HEADS, tm, HEAD_DIM), lambda i: (0, i, 0))
    mem_spec = pl.BlockSpec((None, N_MEM, MEM_WIDTH), lambda i: (i // tiles_per_seq, 0, 0))
    return pl.pallas_call(
        _in_proj_kernel,
        grid=(n // tm,),
        in_specs=[
            pl.BlockSpec((tm, D_MODEL), lambda i: (i, 0)),
            vec(D_MODEL),
            pl.BlockSpec((None, D_MODEL, IN_WIDTH), lambda i: (layer, 0, 0), pipeline_mode=pl.Buffered(1)),
            vec(HEAD_DIM), vec(HEAD_DIM), vec(HEAD_DIM),
            mem_spec, mem_spec,
        ],
        out_specs=[
            head_major, head_major, head_major,
            pl.BlockSpec((tm, MEM_WIDTH), lambda i: (i, 0)),
            pl.BlockSpec((tm, CONV_CH), lambda i: (i, 0)),
        ],
        out_shape=[
            jax.ShapeDtypeStruct((NA_HEADS, n, HEAD_DIM), BF16),
            jax.ShapeDtypeStruct((NA_HEADS, n, HEAD_DIM), BF16),
            jax.ShapeDtypeStruct((NA_HEADS, n, HEAD_DIM), BF16),
            jax.ShapeDtypeStruct((n, MEM_WIDTH), BF16),
            jax.ShapeDtypeStruct((n, CONV_CH), F32),
        ],
        scratch_shapes=[pltpu.VMEM((tm, D_MODEL), BF16), pltpu.VMEM((tm, CONV_CH), F32)],
        compiler_params=pltpu.CompilerParams(vmem_limit_bytes=VMEM_LIMIT),
        name="in_proj",
    )(x, norm_mix_g, w_in, na_q_norm_g, na_k_norm_g, mem_q_norm_g, km, vm)


def _na_kernel(q_ref, k_ref, v_ref, bias_ref, o_ref):
    rows = q_ref.shape[0] // GRID_W

    def row_step(r, carry):
        rs = jnp.clip(r - NA_WIN_ROWS // 2, 0, rows - NA_WIN_ROWS)
        q0 = pl.multiple_of(r * GRID_W, GRID_W)
        k0 = pl.multiple_of(rs * GRID_W, GRID_W)
        q = q_ref[pl.ds(q0, GRID_W), :]
        kw = k_ref[pl.ds(k0, WIN_KEYS), :]
        vw = v_ref[pl.ds(k0, WIN_KEYS), :]
        s = lax.dot_general(q, kw, (((1,), (1,)), ((), ())), preferred_element_type=F32)
        s = s * ATTN_SCALE + bias_ref[r - rs]
        e = jnp.exp(s - jnp.max(s, axis=-1, keepdims=True))
        o = jnp.dot(e.astype(BF16), vw, preferred_element_type=F32)
        o_ref[pl.ds(q0, GRID_W), :] = (o / jnp.sum(e, axis=-1, keepdims=True)).astype(BF16)
        return carry

    lax.fori_loop(0, rows, row_step, 0)


def _na_attention(q, k, v, bias, layer, seq_len):
    n = q.shape[1]
    seq = pl.BlockSpec((None, seq_len, HEAD_DIM), lambda b, h: (h, b, 0))
    return pl.pallas_call(
        _na_kernel,
        grid=(n // seq_len, NA_HEADS),
        in_specs=[seq, seq, seq,
                  pl.BlockSpec((None, None, NA_WIN_ROWS, GRID_W, WIN_KEYS), lambda b, h: (layer, h, 0, 0, 0))],
        out_specs=seq,
        out_shape=jax.ShapeDtypeStruct((NA_HEADS, n, HEAD_DIM), BF16),
        compiler_params=pltpu.CompilerParams(vmem_limit_bytes=VMEM_LIMIT),
        name="na_attn",
    )(q, k, v, bias)


def _out_proj_kernel(x_ref, ona_ref, om_ref, glu_prev, glu_cur, glu_next, cw_ref, cb_ref, lg_ref, lb_ref,
                     w_ref, y_ref, ext_scr, cat_scr, *, tiles_per_seq):
    tm = x_ref.shape[0]
    pos = pl.program_id(0) % tiles_per_seq
    ext_scr[0:HALO, :] = jnp.where(pos == 0, 0.0, glu_prev[...])
    ext_scr[HALO:HALO + tm, :] = glu_cur[...]
    ext_scr[HALO + tm:, :] = jnp.where(pos == tiles_per_seq - 1, 0.0, glu_next[...])
    for hd in range(NA_HEADS):
        cat_scr[:, hd * HEAD_DIM:(hd + 1) * HEAD_DIM] = ona_ref[hd]
    cat_scr[:, NA_WIDTH:NA_WIDTH + MEM_WIDTH] = om_ref[...]

    for r0 in range(0, tm, ROW_CHUNK):
        acc = jnp.broadcast_to(cb_ref[...], (ROW_CHUNK, CONV_CH))
        for tap in range(CONV_K):
            off = r0 + HALO - (CONV_K - 1) // 2 + tap
            acc = acc + ext_scr[off:off + ROW_CHUNK, :] * cw_ref[tap:tap + 1, :]
        mu = jnp.mean(acc, axis=-1, keepdims=True)
        cen = acc - mu
        var = jnp.mean(cen * cen, axis=-1, keepdims=True)
        yln = (cen * lax.rsqrt(var + EPS)) * lg_ref[...] + lb_ref[...]
        cat_scr[r0:r0 + ROW_CHUNK, NA_WIDTH + MEM_WIDTH:] = (yln * jax.nn.sigmoid(yln)).astype(BF16)

    for c in range(D_MODEL // IN_CHUNK):
        sl = slice(c * IN_CHUNK, (c + 1) * IN_CHUNK)
        y_ref[:, sl] = x_ref[:, sl] + jnp.dot(cat_scr[...], w_ref[:, sl], preferred_element_type=F32)


def _out_proj(x, ona, om, glu, layer, seq_len, conv_dw_w, conv_dw_b, conv_ln_g, conv_ln_b, w_out):
    n = x.shape[0]
    tm = TOKEN_TILE
    tiles_per_seq = seq_len // tm
    halo_per_tile = tm // HALO
    last_halo = n // HALO - 1
    vec = lambda width: pl.BlockSpec((None, 1, width), lambda i: (layer, 0, 0))
    return pl.pallas_call(
        functools.partial(_out_proj_kernel, tiles_per_seq=tiles_per_seq),
        grid=(n // tm,),
        in_specs=[
            pl.BlockSpec((tm, D_MODEL), lambda i: (i, 0)),
            pl.BlockSpec((NA_HEADS, tm, HEAD_DIM), lambda i: (0, i, 0)),
            pl.BlockSpec((tm, MEM_WIDTH), lambda i: (i, 0)),
            pl.BlockSpec((HALO, CONV_CH), lambda i: (jnp.maximum(i * halo_per_tile - 1, 0), 0)),
            pl.BlockSpec((tm, CONV_CH), lambda i: (i, 0)),
            pl.BlockSpec((HALO, CONV_CH), lambda i: (jnp.minimum((i + 1) * halo_per_tile, last_halo), 0)),
            pl.BlockSpec((None, CONV_K, CONV_CH), lambda i: (layer, 0, 0)),
            vec(CONV_CH), vec(CONV_CH), vec(CONV_CH),
            pl.BlockSpec((None, D_MODEL, D_MODEL), lambda i: (layer, 0, 0), pipeline_mode=pl.Buffered(1)),
        ],
        out_specs=pl.BlockSpec((tm, D_MODEL), lambda i: (i, 0)),
        out_shape=jax.ShapeDtypeStruct((n, D_MODEL), F32),
        scratch_shapes=[pltpu.VMEM((tm + 2 * HALO, CONV_CH), F32), pltpu.VMEM((tm, D_MODEL), BF16)],
        compiler_params=pltpu.CompilerParams(vmem_limit_bytes=VMEM_LIMIT),
        name="out_proj",
    )(x, ona, om, glu, glu, glu, conv_dw_w, conv_dw_b, conv_ln_g, conv_ln_b, w_out)


def _ffn_kernel(x_prev, x_cur, x_next, ng_ref, wg_ref, wv_ref, cwg_ref, cwv_ref, cbg_ref, cbv_ref, wd_ref,
                y_ref, h_scr, ug_scr, uv_scr, act_scr, *, tiles_per_seq):
    tm = x_cur.shape[0]
    j = pl.program_id(1)

    @pl.when(j == 0)
    def _():
        pos = pl.program_id(0) % tiles_per_seq
        hp = _rms_rows(x_prev[...], ng_ref[...])
        h_scr[0:HALO, :] = jnp.where(pos == 0, 0.0, hp).astype(BF16)
        hn = _rms_rows(x_next[...], ng_ref[...])
        h_scr[HALO + tm:, :] = jnp.where(pos == tiles_per_seq - 1, 0.0, hn).astype(BF16)

        def norm_chunk(c, carry):
            r0 = pl.multiple_of(c * ROW_CHUNK, ROW_CHUNK)
            xc = x_cur[pl.ds(r0, ROW_CHUNK), :]
            h_scr[pl.ds(HALO + r0, ROW_CHUNK), :] = _rms_rows(xc, ng_ref[...]).astype(BF16)
            y_ref[pl.ds(r0, ROW_CHUNK), :] = xc
            return carry

        lax.fori_loop(0, tm // ROW_CHUNK, norm_chunk, 0)

    ug_scr[...] = jnp.dot(h_scr[...], wg_ref[...], preferred_element_type=F32)
    uv_scr[...] = jnp.dot(h_scr[...], wv_ref[...], preferred_element_type=F32)

    def conv3(u_scr, cw_ref, cb_ref, r0):
        acc = jnp.broadcast_to(cb_ref[...], (ROW_CHUNK, FF_BLOCK))
        for tap in range(3):
            off = r0 + HALO - 1 + tap
            acc = acc + u_scr[off:off + ROW_CHUNK, :] * cw_ref[tap:tap + 1, :]
        return acc

    for r0 in range(0, tm, ROW_CHUNK):
        gate = conv3(ug_scr, cwg_ref, cbg_ref, r0)
        val = conv3(uv_scr, cwv_ref, cbv_ref, r0)
        act_scr[r0:r0 + ROW_CHUNK, :] = ((gate * jax.nn.sigmoid(gate)) * val).astype(BF16)
    y_ref[...] += jnp.dot(act_scr[...], wd_ref[...], preferred_element_type=F32)


def _ffn(x, layer, seq_len, norm_ffn_g, w_up, ffn_dw_w, ffn_dw_b, w_down):
    n = x.shape[0]
    tm = TOKEN_TILE
    tiles_per_seq = seq_len // tm
    halo_per_tile = tm // HALO
    last_halo = n // HALO - 1
    nblk = D_FF // FF_BLOCK
    return pl.pallas_call(
        functools.partial(_ffn_kernel, tiles_per_seq=tiles_per_seq),
        grid=(n // tm, nblk),
        in_specs=[
            pl.BlockSpec((HALO, D_MODEL), lambda i, j: (jnp.maximum(i * halo_per_tile - 1, 0), 0)),
            pl.BlockSpec((tm, D_MODEL), lambda i, j: (i, 0)),
            pl.BlockSpec((HALO, D_MODEL), lambda i, j: (jnp.minimum((i + 1) * halo_per_tile, last_halo), 0)),
            pl.BlockSpec((None, 1, D_MODEL), lambda i, j: (layer, 0, 0)),
            pl.BlockSpec((None, D_MODEL, FF_BLOCK), lambda i, j: (layer, 0, j)),
            pl.BlockSpec((None, D_MODEL, FF_BLOCK), lambda i, j: (layer, 0, nblk + j)),
            pl.BlockSpec((None, 3, FF_BLOCK), lambda i, j: (layer, 0, j)),
            pl.BlockSpec((None, 3, FF_BLOCK), lambda i, j: (layer, 0, nblk + j)),
            pl.BlockSpec((None, 1, FF_BLOCK), lambda i, j: (layer, 0, j)),
            pl.BlockSpec((None, 1, FF_BLOCK), lambda i, j: (layer, 0, nblk + j)),
            pl.BlockSpec((None, FF_BLOCK, D_MODEL), lambda i, j: (layer, j, 0)),
        ],
        out_specs=pl.BlockSpec((tm, D_MODEL), lambda i, j: (i, 0)),
        out_shape=jax.ShapeDtypeStruct((n, D_MODEL), F32),
        scratch_shapes=[
            pltpu.VMEM((tm + 2 * HALO, D_MODEL), BF16),
            pltpu.VMEM((tm + 2 * HALO, FF_BLOCK), F32),
            pltpu.VMEM((tm + 2 * HALO, FF_BLOCK), F32),
            pltpu.VMEM((tm, FF_BLOCK), BF16),
        ],
        compiler_params=pltpu.CompilerParams(
            dimension_semantics=("arbitrary", "arbitrary"), vmem_limit_bytes=VMEM_LIMIT),
        name="ffn",
    )(x, x, x, norm_ffn_g, w_up, w_up, ffn_dw_w, ffn_dw_w, ffn_dw_b, ffn_dw_b, w_down)


def kernel(x_prompt, x_sample, mem_prompt, mem_sample, norm_mix_g, w_in, na_q_norm_g, na_k_norm_g, na_rpb,
           mem_norm_g, w_mem_kv, mem_q_norm_g, mem_k_norm_g, conv_dw_w, conv_dw_b, conv_ln_g, conv_ln_b, w_out,
           norm_ffn_g, w_up, ffn_dw_w, ffn_dw_b, w_down):
    depth = w_in.shape[0]
    row = lambda a: a.reshape(depth, 1, a.shape[-1])
    w_in_b, w_kv_b, w_out_b, w_up_b, w_down_b = (w.astype(BF16) for w in (w_in, w_mem_kv, w_out, w_up, w_down))
    norm_mix_g, na_q_norm_g, na_k_norm_g, mem_norm_g, mem_q_norm_g, mem_k_norm_g = map(
        row, (norm_mix_g, na_q_norm_g, na_k_norm_g, mem_norm_g, mem_q_norm_g, mem_k_norm_g))
    conv_dw_b, conv_ln_g, conv_ln_b, norm_ffn_g, ffn_dw_b = map(
        row, (conv_dw_b, conv_ln_g, conv_ln_b, norm_ffn_g, ffn_dw_b))
    bias = _bias_tables(na_rpb)

    def trunk(x3, mem):
        b, seq_len, _ = x3.shape
        x = x3.reshape(b * seq_len, D_MODEL)
        for layer in range(depth):
            km, vm = _mem_kv(mem, layer, mem_norm_g, w_kv_b, mem_k_norm_g)
            q, k, v, om, glu = _in_proj(x, layer, seq_len, km, vm, norm_mix_g, w_in_b,
                                        na_q_norm_g, na_k_norm_g, mem_q_norm_g)
            ona = _na_attention(q, k, v, bias, layer, seq_len)
            x = _out_proj(x, ona, om, glu, layer, seq_len, conv_dw_w, conv_dw_b, conv_ln_g, conv_ln_b, w_out_b)
            x = _ffn(x, layer, seq_len, norm_ffn_g, w_up_b, ffn_dw_w, ffn_dw_b, w_down_b)
        return x.reshape(b, seq_len, D_MODEL)

    return trunk(x_prompt, mem_prompt), trunk(x_sample, mem_sample)
```

```python
import functools

import jax
import jax.numpy as jnp
from jax import lax
from jax.experimental import pallas as pl
from jax.experimental.pallas import tpu as pltpu

F32 = jnp.float32
BF16 = jnp.bfloat16

D_MODEL = 2048
GRID_W = 64
HEAD_DIM = 128
NA_HEADS = 8
NA_WIDTH = NA_HEADS * HEAD_DIM
NA_WIN_ROWS = 8
NA_WIN_COLS = 16
RPB_ROWS = 2 * NA_WIN_ROWS - 1
RPB_COLS = 2 * NA_WIN_COLS - 1
MEM_HEADS = 4
MEM_WIDTH = MEM_HEADS * HEAD_DIM
N_MEM = 256
CONV_CH = D_MODEL - NA_WIDTH - MEM_WIDTH
CONV_K = 31
D_FF = 5632
IN_WIDTH = 3 * NA_WIDTH + MEM_WIDTH + 2 * CONV_CH
EPS = 1e-6
ATTN_SCALE = HEAD_DIM ** -0.5
MASK_VALUE = -1e30

WIN_KEYS = NA_WIN_ROWS * GRID_W
TOKEN_TILE = 512
HALO = 16
IN_CHUNK = 512
FF_BLOCK = 512
ROW_CHUNK = 32
NORM_UNROLL = 4
NA_ROWS_PER_STEP = 16
SUBLANES = 8
CONV_ROWS = 16
VMEM_LIMIT = 56 * 1024 * 1024


def _rms_rows(x, g):
    ms = jnp.mean(x * x, axis=-1, keepdims=True)
    return (x * lax.rsqrt(ms + EPS)) * g


def _bias_kernel(rpb_ref, out_ref):
    l = pl.program_id(0)
    h = pl.program_id(1)
    oi = pl.program_id(2)
    shape = (GRID_W, WIN_KEYS)
    q = lax.broadcasted_iota(jnp.int32, shape, 0)
    lane = lax.broadcasted_iota(jnp.int32, shape, 1)
    kc = lane & (GRID_W - 1)
    d = kc - q + (NA_WIN_COLS - 1)
    cs = jnp.clip(q - NA_WIN_COLS // 2, 0, GRID_W - NA_WIN_COLS)
    valid = (kc >= cs) & (kc < cs + NA_WIN_COLS)
    lane_w = lax.broadcasted_iota(jnp.int32, (1, WIN_KEYS), 1) >> 6
    base = (l * NA_HEADS + h) * (RPB_ROWS * RPB_COLS)
    t = jnp.full(shape, MASK_VALUE, F32)
    for dd in range(RPB_COLS):
        row = jnp.zeros((1, WIN_KEYS), F32)
        for w in range(NA_WIN_ROWS):
            val = rpb_ref[base + (w - oi + NA_WIN_ROWS - 1) * RPB_COLS + dd]
            row = jnp.where(lane_w == w, val, row)
        t = jnp.where(valid & (d == dd), jnp.broadcast_to(row, shape), t)
    out_ref[...] = t


def _bias_tables(na_rpb):
    depth = na_rpb.shape[0]
    return pl.pallas_call(
        _bias_kernel,
        grid=(depth, NA_HEADS, NA_WIN_ROWS),
        in_specs=[pl.BlockSpec(memory_space=pltpu.SMEM)],
        out_specs=pl.BlockSpec((None, None, None, GRID_W, WIN_KEYS), lambda l, h, o: (l, h, o, 0, 0)),
        out_shape=jax.ShapeDtypeStruct((depth, NA_HEADS, NA_WIN_ROWS, GRID_W, WIN_KEYS), F32),
        name="rpb_tables",
    )(na_rpb.reshape(-1))


def _mem_kv_kernel(mem_ref, g_ref, w_ref, kg_ref, k_out, v_out):
    h = _rms_rows(mem_ref[...], g_ref[...]).astype(BF16)
    kv = jnp.dot(h, w_ref[...], preferred_element_type=F32)
    for hd in range(MEM_HEADS):
        sl = slice(hd * HEAD_DIM, (hd + 1) * HEAD_DIM)
        k_out[:, sl] = _rms_rows(kv[:, sl], kg_ref[...]).astype(BF16)
    v_out[...] = kv[:, MEM_WIDTH:].astype(BF16)


def _mem_kv(mem, layer, mem_norm_g, w_mem_kv, mem_k_norm_g):
    b = mem.shape[0]
    return pl.pallas_call(
        _mem_kv_kernel,
        grid=(b,),
        in_specs=[
            pl.BlockSpec((None, N_MEM, D_MODEL), lambda i: (i, 0, 0)),
            pl.BlockSpec((None, 1, D_MODEL), lambda i: (layer, 0, 0)),
            pl.BlockSpec((None, D_MODEL, 2 * MEM_WIDTH), lambda i: (layer, 0, 0)),
            pl.BlockSpec((None, 1, HEAD_DIM), lambda i: (layer, 0, 0)),
        ],
        out_specs=[
            pl.BlockSpec((None, N_MEM, MEM_WIDTH), lambda i: (i, 0, 0)),
            pl.BlockSpec((None, N_MEM, MEM_WIDTH), lambda i: (i, 0, 0)),
        ],
        out_shape=[jax.ShapeDtypeStruct((b, N_MEM, MEM_WIDTH), BF16)] * 2,
        compiler_params=pltpu.CompilerParams(vmem_limit_bytes=VMEM_LIMIT),
        name="mem_kv",
    )(mem, mem_norm_g, w_mem_kv, mem_k_norm_g)


def _in_proj_kernel(x_ref, ng_ref, w_ref, qg_ref, kg_ref, mqg_ref, km_ref, vm_ref,
                    q_out, k_out, v_out, om_out, glu_out, h_scr, u_scr):
    tm = x_ref.shape[0]

    def norm_chunk(c, carry):
        r0 = pl.multiple_of(c * ROW_CHUNK, ROW_CHUNK)
        h_scr[pl.ds(r0, ROW_CHUNK), :] = _rms_rows(x_ref[pl.ds(r0, ROW_CHUNK), :], ng_ref[...]).astype(BF16)
        return carry

    lax.fori_loop(0, tm // ROW_CHUNK, norm_chunk, 0, unroll=NORM_UNROLL)

    heads_per_chunk = IN_CHUNK // HEAD_DIM
    for c in range(IN_WIDTH // IN_CHUNK):
        z = jnp.dot(h_scr[...], w_ref[:, c * IN_CHUNK:(c + 1) * IN_CHUNK], preferred_element_type=F32)
        col = c * IN_CHUNK
        if col < 2 * NA_WIDTH:
            is_q = col < NA_WIDTH
            out, g_ref = (q_out, qg_ref) if is_q else (k_out, kg_ref)
            head0 = (col - (0 if is_q else NA_WIDTH)) // HEAD_DIM
            for j in range(heads_per_chunk):
                zh = z[:, j * HEAD_DIM:(j + 1) * HEAD_DIM]
                out[head0 + j] = _rms_rows(zh, g_ref[...]).astype(BF16)
        elif col < 3 * NA_WIDTH:
            head0 = (col - 2 * NA_WIDTH) // HEAD_DIM
            for j in range(heads_per_chunk):
                v_out[head0 + j] = z[:, j * HEAD_DIM:(j + 1) * HEAD_DIM].astype(BF16)
        elif col < 3 * NA_WIDTH + MEM_WIDTH:
            for j in range(MEM_HEADS):
                sl = slice(j * HEAD_DIM, (j + 1) * HEAD_DIM)
                qh = _rms_rows(z[:, sl], mqg_ref[...]).astype(BF16)
                s = lax.dot_general(qh, km_ref[:, sl], (((1,), (1,)), ((), ())),
                                    preferred_element_type=F32) * ATTN_SCALE
                e = jnp.exp(s - jnp.max(s, axis=-1, keepdims=True))
                o = jnp.dot(e.astype(BF16), vm_ref[:, sl], preferred_element_type=F32)
                om_out[:, sl] = (o / jnp.sum(e, axis=-1, keepdims=True)).astype(BF16)
        elif col < 3 * NA_WIDTH + MEM_WIDTH + CONV_CH:
            u_scr[...] = z
        else:
            glu_out[...] = u_scr[...] * jax.nn.sigmoid(z)


def _in_proj(x, layer, seq_len, km, vm, norm_mix_g, w_in, na_q_norm_g, na_k_norm_g, mem_q_norm_g):
    n = x.shape[0]
    tm = TOKEN_TILE
    tiles_per_seq = seq_len // tm
    vec = lambda width: pl.BlockSpec((None, 1, width), lambda i: (layer, 0, 0))
    head_major = pl.BlockSpec((NA_HEADS, tm, HEAD_DIM), lambda i: (0, i, 0))
    mem_spec = pl.BlockSpec((None, N_MEM, MEM_WIDTH), lambda i: (i // tiles_per_seq, 0, 0))
    return pl.pallas_call(
        _in_proj_kernel,
        grid=(n // tm,),
        in_specs=[
            pl.BlockSpec((tm, D_MODEL), lambda i: (i, 0)),
            vec(D_MODEL),
            pl.BlockSpec((None, D_MODEL, IN_WIDTH), lambda i: (layer, 0, 0), pipeline_mode=pl.Buffered(1)),
            vec(HEAD_DIM), vec(HEAD_DIM), vec(HEAD_DIM),
            mem_spec, mem_spec,
        ],
        out_specs=[
            head_major, head_major, head_major,
            pl.BlockSpec((tm, MEM_WIDTH), lambda i: (i, 0)),
            pl.BlockSpec((tm, CONV_CH), lambda i: (i, 0)),
        ],
        out_shape=[
            jax.ShapeDtypeStruct((NA_HEADS, n, HEAD_DIM), BF16),
            jax.ShapeDtypeStruct((NA_HEADS, n, HEAD_DIM), BF16),
            jax.ShapeDtypeStruct((NA_HEADS, n, HEAD_DIM), BF16),
            jax.ShapeDtypeStruct((n, MEM_WIDTH), BF16),
            jax.ShapeDtypeStruct((n, CONV_CH), F32),
        ],
        scratch_shapes=[pltpu.VMEM((tm, D_MODEL), BF16), pltpu.VMEM((tm, CONV_CH), F32)],
        compiler_params=pltpu.CompilerParams(vmem_limit_bytes=VMEM_LIMIT),
        name="in_proj",
    )(x, norm_mix_g, w_in, na_q_norm_g, na_k_norm_g, mem_q_norm_g, km, vm)


def _na_kernel(q_ref, k_ref, v_ref, bias_ref, o_ref):
    rows = q_ref.shape[0] // GRID_W

    def group_step(g, carry):
        row_ids = [g * NA_ROWS_PER_STEP + u for u in range(NA_ROWS_PER_STEP)]
        starts = [jnp.clip(r - NA_WIN_ROWS // 2, 0, rows - NA_WIN_ROWS) for r in row_ids]
        q0s = [pl.multiple_of(r * GRID_W, GRID_W) for r in row_ids]
        k0s = [pl.multiple_of(rs * GRID_W, GRID_W) for rs in starts]
        scores = []
        for q0, k0 in zip(q0s, k0s):
            q = q_ref[pl.ds(q0, GRID_W), :]
            kw = k_ref[pl.ds(k0, WIN_KEYS), :]
            scores.append(lax.dot_general(q, kw, (((1,), (1,)), ((), ())), preferred_element_type=F32))
        probs, sums = [], []
        for r, rs, s in zip(row_ids, starts, scores):
            s = s * ATTN_SCALE + bias_ref[r - rs]
            e = jnp.exp(s - jnp.max(s, axis=-1, keepdims=True))
            probs.append(e.astype(BF16))
            sums.append(jnp.sum(e, axis=-1, keepdims=True))
        for q0, k0, p, l in zip(q0s, k0s, probs, sums):
            o = jnp.dot(p, v_ref[pl.ds(k0, WIN_KEYS), :], preferred_element_type=F32)
            o_ref[pl.ds(q0, GRID_W), :] = (o / l).astype(BF16)
        return carry

    lax.fori_loop(0, rows // NA_ROWS_PER_STEP, group_step, 0)


def _na_attention(q, k, v, bias, layer, seq_len):
    n = q.shape[1]
    seq = pl.BlockSpec((None, seq_len, HEAD_DIM), lambda b, h: (h, b, 0))
    return pl.pallas_call(
        _na_kernel,
        grid=(n // seq_len, NA_HEADS),
        in_specs=[seq, seq, seq,
                  pl.BlockSpec((None, None, NA_WIN_ROWS, GRID_W, WIN_KEYS), lambda b, h: (layer, h, 0, 0, 0))],
        out_specs=seq,
        out_shape=jax.ShapeDtypeStruct((NA_HEADS, n, HEAD_DIM), BF16),
        compiler_params=pltpu.CompilerParams(vmem_limit_bytes=VMEM_LIMIT),
        name="na_attn",
    )(q, k, v, bias)


def _out_proj_kernel(x_ref, ona_ref, om_ref, glu_prev, glu_cur, glu_next, cw_ref, cb_ref, lg_ref, lb_ref,
                     w_ref, y_ref, ext_scr, shift_scr, cat_scr, *, tiles_per_seq):
    tm = x_ref.shape[0]
    attn_width = NA_WIDTH + MEM_WIDTH
    pos = pl.program_id(0) % tiles_per_seq
    ext_scr[0:HALO, :] = jnp.where(pos == 0, 0.0, glu_prev[...])
    ext_scr[HALO:HALO + tm, :] = glu_cur[...]
    ext_scr[HALO + tm:, :] = jnp.where(pos == tiles_per_seq - 1, 0.0, glu_next[...])
    for hd in range(NA_HEADS):
        cat_scr[:, hd * HEAD_DIM:(hd + 1) * HEAD_DIM] = ona_ref[hd]
    cat_scr[:, NA_WIDTH:attn_width] = om_ref[...]

    for c in range(D_MODEL // IN_CHUNK):
        sl = slice(c * IN_CHUNK, (c + 1) * IN_CHUNK)
        y_ref[:, sl] = x_ref[:, sl] + jnp.dot(cat_scr[:, :attn_width], w_ref[:attn_width, sl],
                                              preferred_element_type=F32)

    lead = HALO - (CONV_K - 1) // 2
    span = shift_scr.shape[1]
    for b in range(1, SUBLANES):
        shift_scr[b - 1] = ext_scr[b:b + span, :]
    rep = lambda v: jnp.concatenate([v] * (CONV_ROWS // SUBLANES), axis=0)
    for r0 in range(0, tm, CONV_ROWS):
        acc = rep(cb_ref[...])
        for tap in range(CONV_K):
            a, b = divmod(tap + lead, SUBLANES)
            off = r0 + SUBLANES * a
            win = ext_scr[off:off + CONV_ROWS, :] if b == 0 else shift_scr[b - 1, off:off + CONV_ROWS, :]
            acc = acc + win * rep(cw_ref[tap])
        mu = jnp.mean(acc, axis=-1, keepdims=True)
        cen = acc - mu
        var = jnp.mean(cen * cen, axis=-1, keepdims=True)
        yln = (cen * lax.rsqrt(var + EPS)) * rep(lg_ref[...]) + rep(lb_ref[...])
        cat_scr[r0:r0 + CONV_ROWS, attn_width:] = (yln * jax.nn.sigmoid(yln)).astype(BF16)

    for c in range(D_MODEL // IN_CHUNK):
        sl = slice(c * IN_CHUNK, (c + 1) * IN_CHUNK)
        y_ref[:, sl] += jnp.dot(cat_scr[:, attn_width:], w_ref[attn_width:, sl], preferred_element_type=F32)


def _out_proj(x, ona, om, glu, layer, seq_len, conv_dw_w, conv_dw_b, conv_ln_g, conv_ln_b, w_out):
    n = x.shape[0]
    tm = TOKEN_TILE
    tiles_per_seq = seq_len // tm
    halo_per_tile = tm // HALO
    last_halo = n // HALO - 1
    sub_vec = pl.BlockSpec((None, SUBLANES, CONV_CH), lambda i: (layer, 0, 0))
    return pl.pallas_call(
        functools.partial(_out_proj_kernel, tiles_per_seq=tiles_per_seq),
        grid=(n // tm,),
        in_specs=[
            pl.BlockSpec((tm, D_MODEL), lambda i: (i, 0)),
            pl.BlockSpec((NA_HEADS, tm, HEAD_DIM), lambda i: (0, i, 0)),
            pl.BlockSpec((tm, MEM_WIDTH), lambda i: (i, 0)),
            pl.BlockSpec((HALO, CONV_CH), lambda i: (jnp.maximum(i * halo_per_tile - 1, 0), 0)),
            pl.BlockSpec((tm, CONV_CH), lambda i: (i, 0)),
            pl.BlockSpec((HALO, CONV_CH), lambda i: (jnp.minimum((i + 1) * halo_per_tile, last_halo), 0)),
            pl.BlockSpec((None, CONV_K, SUBLANES, CONV_CH), lambda i: (layer, 0, 0, 0)),
            sub_vec, sub_vec, sub_vec,
            pl.BlockSpec((None, D_MODEL, D_MODEL), lambda i: (layer, 0, 0), pipeline_mode=pl.Buffered(1)),
        ],
        out_specs=pl.BlockSpec((tm, D_MODEL), lambda i: (i, 0)),
        out_shape=jax.ShapeDtypeStruct((n, D_MODEL), F32),
        scratch_shapes=[
            pltpu.VMEM((tm + 2 * HALO, CONV_CH), F32),
            pltpu.VMEM((SUBLANES - 1, tm + 2 * HALO - SUBLANES, CONV_CH), F32),
            pltpu.VMEM((tm, D_MODEL), BF16),
        ],
        compiler_params=pltpu.CompilerParams(vmem_limit_bytes=VMEM_LIMIT),
        name="out_proj",
    )(x, ona, om, glu, glu, glu, conv_dw_w, conv_dw_b, conv_ln_g, conv_ln_b, w_out)


def _ffn_kernel(x_prev, x_cur, x_next, ng_ref, wg_ref, wv_ref, cwg_ref, cwv_ref, cbg_ref, cbv_ref, wd_ref,
                y_ref, h_scr, ug_scr, uv_scr, act_scr, *, tiles_per_seq):
    tm = x_cur.shape[0]
    j = pl.program_id(1)

    @pl.when(j == 0)
    def _():
        pos = pl.program_id(0) % tiles_per_seq
        hp = _rms_rows(x_prev[...], ng_ref[...])
        h_scr[0:HALO, :] = jnp.where(pos == 0, 0.0, hp).astype(BF16)
        hn = _rms_rows(x_next[...], ng_ref[...])
        h_scr[HALO + tm:, :] = jnp.where(pos == tiles_per_seq - 1, 0.0, hn).astype(BF16)

        def norm_chunk(c, carry):
            r0 = pl.multiple_of(c * ROW_CHUNK, ROW_CHUNK)
            xc = x_cur[pl.ds(r0, ROW_CHUNK), :]
            h_scr[pl.ds(HALO + r0, ROW_CHUNK), :] = _rms_rows(xc, ng_ref[...]).astype(BF16)
            y_ref[pl.ds(r0, ROW_CHUNK), :] = xc
            return carry

        lax.fori_loop(0, tm // ROW_CHUNK, norm_chunk, 0, unroll=NORM_UNROLL)

    ug_scr[...] = jnp.dot(h_scr[...], wg_ref[...], preferred_element_type=F32)
    uv_scr[...] = jnp.dot(h_scr[...], wv_ref[...], preferred_element_type=F32)

    def conv3(u_scr, cw_ref, cb_ref, r0):
        acc = jnp.broadcast_to(cb_ref[...], (ROW_CHUNK, FF_BLOCK))
        for tap in range(3):
            off = r0 + HALO - 1 + tap
            acc = acc + u_scr[off:off + ROW_CHUNK, :] * cw_ref[tap:tap + 1, :]
        return acc

    half = tm // 2
    for lo in (0, half):
        for r0 in range(lo, lo + half, ROW_CHUNK):
            gate = conv3(ug_scr, cwg_ref, cbg_ref, r0)
            val = conv3(uv_scr, cwv_ref, cbv_ref, r0)
            act_scr[r0:r0 + ROW_CHUNK, :] = ((gate * jax.nn.sigmoid(gate)) * val).astype(BF16)
        y_ref[lo:lo + half, :] += jnp.dot(act_scr[lo:lo + half, :], wd_ref[...], preferred_element_type=F32)


def _ffn(x, layer, seq_len, norm_ffn_g, w_up, ffn_dw_w, ffn_dw_b, w_down):
    n = x.shape[0]
    tm = TOKEN_TILE
    tiles_per_seq = seq_len // tm
    halo_per_tile = tm // HALO
    last_halo = n // HALO - 1
    nblk = D_FF // FF_BLOCK
    return pl.pallas_call(
        functools.partial(_ffn_kernel, tiles_per_seq=tiles_per_seq),
        grid=(n // tm, nblk),
        in_specs=[
            pl.BlockSpec((HALO, D_MODEL), lambda i, j: (jnp.maximum(i * halo_per_tile - 1, 0), 0)),
            pl.BlockSpec((tm, D_MODEL), lambda i, j: (i, 0)),
            pl.BlockSpec((HALO, D_MODEL), lambda i, j: (jnp.minimum((i + 1) * halo_per_tile, last_halo), 0)),
            pl.BlockSpec((None, 1, D_MODEL), lambda i, j: (layer, 0, 0)),
            pl.BlockSpec((None, D_MODEL, FF_BLOCK), lambda i, j: (layer, 0, j)),
            pl.BlockSpec((None, D_MODEL, FF_BLOCK), lambda i, j: (layer, 0, nblk + j)),
            pl.BlockSpec((None, 3, FF_BLOCK), lambda i, j: (layer, 0, j)),
            pl.BlockSpec((None, 3, FF_BLOCK), lambda i, j: (layer, 0, nblk + j)),
            pl.BlockSpec((None, 1, FF_BLOCK), lambda i, j: (layer, 0, j)),
            pl.BlockSpec((None, 1, FF_BLOCK), lambda i, j: (layer, 0, nblk + j)),
            pl.BlockSpec((None, FF_BLOCK, D_MODEL), lambda i, j: (layer, j, 0)),
        ],
        out_specs=pl.BlockSpec((tm, D_MODEL), lambda i, j: (i, 0)),
        out_shape=jax.ShapeDtypeStruct((n, D_MODEL), F32),
        scratch_shapes=[
            pltpu.VMEM((tm + 2 * HALO, D_MODEL), BF16),
            pltpu.VMEM((tm + 2 * HALO, FF_BLOCK), F32),
            pltpu.VMEM((tm + 2 * HALO, FF_BLOCK), F32),
            pltpu.VMEM((tm, FF_BLOCK), BF16),
        ],
        compiler_params=pltpu.CompilerParams(
            dimension_semantics=("arbitrary", "arbitrary"), vmem_limit_bytes=VMEM_LIMIT),
        name="ffn",
    )(x, x, x, norm_ffn_g, w_up, w_up, ffn_dw_w, ffn_dw_w, ffn_dw_b, ffn_dw_b, w_down)


def kernel(x_prompt, x_sample, mem_prompt, mem_sample, norm_mix_g, w_in, na_q_norm_g, na_k_norm_g, na_rpb,
           mem_norm_g, w_mem_kv, mem_q_norm_g, mem_k_norm_g, conv_dw_w, conv_dw_b, conv_ln_g, conv_ln_b, w_out,
           norm_ffn_g, w_up, ffn_dw_w, ffn_dw_b, w_down):
    depth = w_in.shape[0]
    row = lambda a: a.reshape(depth, 1, a.shape[-1])
    w_in_b, w_kv_b, w_out_b, w_up_b, w_down_b = (w.astype(BF16) for w in (w_in, w_mem_kv, w_out, w_up, w_down))
    norm_mix_g, na_q_norm_g, na_k_norm_g, mem_norm_g, mem_q_norm_g, mem_k_norm_g = map(
        row, (norm_mix_g, na_q_norm_g, na_k_norm_g, mem_norm_g, mem_q_norm_g, mem_k_norm_g))
    norm_ffn_g, ffn_dw_b = map(row, (norm_ffn_g, ffn_dw_b))
    sub = lambda a: jnp.broadcast_to(a[..., None, :], a.shape[:-1] + (SUBLANES, a.shape[-1]))
    conv_dw_w, conv_dw_b, conv_ln_g, conv_ln_b = map(sub, (conv_dw_w, conv_dw_b, conv_ln_g, conv_ln_b))
    bias = _bias_tables(na_rpb)

    def trunk(x3, mem):
        b, seq_len, _ = x3.shape
        x = x3.reshape(b * seq_len, D_MODEL)
        for layer in range(depth):
            km, vm = _mem_kv(mem, layer, mem_norm_g, w_kv_b, mem_k_norm_g)
            q, k, v, om, glu = _in_proj(x, layer, seq_len, km, vm, norm_mix_g, w_in_b,
                                        na_q_norm_g, na_k_norm_g, mem_q_norm_g)
            ona = _na_attention(q, k, v, bias, layer, seq_len)
            x = _out_proj(x, ona, om, glu, layer, seq_len, conv_dw_w, conv_dw_b, conv_ln_g, conv_ln_b, w_out_b)
            x = _ffn(x, layer, seq_len, norm_ffn_g, w_up_b, ffn_dw_w, ffn_dw_b, w_down_b)
        return x.reshape(b, seq_len, D_MODEL)

    return trunk(x_prompt, mem_prompt), trunk(x_sample, mem_sample)
```

```python
import functools

import jax
import jax.numpy as jnp
from jax import lax
from jax.experimental import pallas as pl
from jax.experimental.pallas import tpu as pltpu

F32 = jnp.float32
BF16 = jnp.bfloat16

D_MODEL = 2048
GRID_W = 64
HEAD_DIM = 128
NA_HEADS = 8
NA_WIDTH = NA_HEADS * HEAD_DIM
NA_WIN_ROWS = 8
NA_WIN_COLS = 16
RPB_ROWS = 2 * NA_WIN_ROWS - 1
RPB_COLS = 2 * NA_WIN_COLS - 1
MEM_HEADS = 4
MEM_WIDTH = MEM_HEADS * HEAD_DIM
N_MEM = 256
CONV_CH = D_MODEL - NA_WIDTH - MEM_WIDTH
CONV_K = 31
D_FF = 5632
IN_WIDTH = 3 * NA_WIDTH + MEM_WIDTH + 2 * CONV_CH
EPS = 1e-6
ATTN_SCALE = HEAD_DIM ** -0.5
MASK_VALUE = -1e30

WIN_KEYS = NA_WIN_ROWS * GRID_W
TOKEN_TILE = 512
FFN_TOKEN_TILE = 1024
HALO = 16
IN_CHUNK = 512
FF_BLOCK = 512
ROW_CHUNK = 32
NORM_UNROLL = 4
NA_ROWS_PER_STEP = 16
SUBLANES = 8
FFN_ROW_GROUPS = 2
CONV_ROWS = 16
VMEM_LIMIT = 60 * 1024 * 1024


def _rms_rows(x, g):
    ms = jnp.mean(x * x, axis=-1, keepdims=True)
    return (x * lax.rsqrt(ms + EPS)) * g


def _bias_kernel(rpb_ref, out_ref):
    l = pl.program_id(0)
    h = pl.program_id(1)
    oi = pl.program_id(2)
    shape = (GRID_W, WIN_KEYS)
    q = lax.broadcasted_iota(jnp.int32, shape, 0)
    lane = lax.broadcasted_iota(jnp.int32, shape, 1)
    kc = lane & (GRID_W - 1)
    d = kc - q + (NA_WIN_COLS - 1)
    cs = jnp.clip(q - NA_WIN_COLS // 2, 0, GRID_W - NA_WIN_COLS)
    valid = (kc >= cs) & (kc < cs + NA_WIN_COLS)
    lane_w = lax.broadcasted_iota(jnp.int32, (1, WIN_KEYS), 1) >> 6
    base = (l * NA_HEADS + h) * (RPB_ROWS * RPB_COLS)
    t = jnp.full(shape, MASK_VALUE, F32)
    for dd in range(RPB_COLS):
        row = jnp.zeros((1, WIN_KEYS), F32)
        for w in range(NA_WIN_ROWS):
            val = rpb_ref[base + (w - oi + NA_WIN_ROWS - 1) * RPB_COLS + dd]
            row = jnp.where(lane_w == w, val, row)
        t = jnp.where(valid & (d == dd), jnp.broadcast_to(row, shape), t)
    out_ref[...] = t


def _bias_tables(na_rpb):
    depth = na_rpb.shape[0]
    return pl.pallas_call(
        _bias_kernel,
        grid=(depth, NA_HEADS, NA_WIN_ROWS),
        in_specs=[pl.BlockSpec(memory_space=pltpu.SMEM)],
        out_specs=pl.BlockSpec((None, None, None, GRID_W, WIN_KEYS), lambda l, h, o: (l, h, o, 0, 0)),
        out_shape=jax.ShapeDtypeStruct((depth, NA_HEADS, NA_WIN_ROWS, GRID_W, WIN_KEYS), F32),
        name="rpb_tables",
    )(na_rpb.reshape(-1))


def _mem_kv_kernel(mem_ref, g_ref, w_ref, kg_ref, k_out, v_out):
    h = _rms_rows(mem_ref[...], g_ref[...]).astype(BF16)
    kv = jnp.dot(h, w_ref[...], preferred_element_type=F32)
    for hd in range(MEM_HEADS):
        sl = slice(hd * HEAD_DIM, (hd + 1) * HEAD_DIM)
        k_out[:, sl] = _rms_rows(kv[:, sl], kg_ref[...]).astype(BF16)
    v_out[...] = kv[:, MEM_WIDTH:].astype(BF16)


def _mem_kv(mem, layer, mem_norm_g, w_mem_kv, mem_k_norm_g):
    b = mem.shape[0]
    return pl.pallas_call(
        _mem_kv_kernel,
        grid=(b,),
        in_specs=[
            pl.BlockSpec((None, N_MEM, D_MODEL), lambda i: (i, 0, 0)),
            pl.BlockSpec((None, 1, D_MODEL), lambda i: (layer, 0, 0)),
            pl.BlockSpec((None, D_MODEL, 2 * MEM_WIDTH), lambda i: (layer, 0, 0)),
            pl.BlockSpec((None, 1, HEAD_DIM), lambda i: (layer, 0, 0)),
        ],
        out_specs=[
            pl.BlockSpec((None, N_MEM, MEM_WIDTH), lambda i: (i, 0, 0)),
            pl.BlockSpec((None, N_MEM, MEM_WIDTH), lambda i: (i, 0, 0)),
        ],
        out_shape=[jax.ShapeDtypeStruct((b, N_MEM, MEM_WIDTH), BF16)] * 2,
        compiler_params=pltpu.CompilerParams(vmem_limit_bytes=VMEM_LIMIT),
        name="mem_kv",
    )(mem, mem_norm_g, w_mem_kv, mem_k_norm_g)


def _in_proj_kernel(x_ref, ng_ref, w_ref, qg_ref, kg_ref, mqg_ref, km_ref, vm_ref,
                    q_out, k_out, v_out, om_out, glu_out, h_scr, u_scr):
    tm = x_ref.shape[0]

    def norm_chunk(c, carry):
        r0 = pl.multiple_of(c * ROW_CHUNK, ROW_CHUNK)
        h_scr[pl.ds(r0, ROW_CHUNK), :] = _rms_rows(x_ref[pl.ds(r0, ROW_CHUNK), :], ng_ref[...]).astype(BF16)
        return carry

    lax.fori_loop(0, tm // ROW_CHUNK, norm_chunk, 0, unroll=NORM_UNROLL)

    heads_per_chunk = IN_CHUNK // HEAD_DIM
    for c in range(IN_WIDTH // IN_CHUNK):
        z = jnp.dot(h_scr[...], w_ref[:, c * IN_CHUNK:(c + 1) * IN_CHUNK], preferred_element_type=F32)
        col = c * IN_CHUNK
        if col < 2 * NA_WIDTH:
            is_q = col < NA_WIDTH
            out, g_ref = (q_out, qg_ref) if is_q else (k_out, kg_ref)
            head0 = (col - (0 if is_q else NA_WIDTH)) // HEAD_DIM
            for j in range(heads_per_chunk):
                zh = z[:, j * HEAD_DIM:(j + 1) * HEAD_DIM]
                out[head0 + j] = _rms_rows(zh, g_ref[...]).astype(BF16)
        elif col < 3 * NA_WIDTH:
            head0 = (col - 2 * NA_WIDTH) // HEAD_DIM
            for j in range(heads_per_chunk):
                v_out[head0 + j] = z[:, j * HEAD_DIM:(j + 1) * HEAD_DIM].astype(BF16)
        elif col < 3 * NA_WIDTH + MEM_WIDTH:
            for j in range(MEM_HEADS):
                sl = slice(j * HEAD_DIM, (j + 1) * HEAD_DIM)
                qh = _rms_rows(z[:, sl], mqg_ref[...]).astype(BF16)
                s = lax.dot_general(qh, km_ref[:, sl], (((1,), (1,)), ((), ())),
                                    preferred_element_type=F32) * ATTN_SCALE
                e = jnp.exp(s - jnp.max(s, axis=-1, keepdims=True))
                o = jnp.dot(e.astype(BF16), vm_ref[:, sl], preferred_element_type=F32)
                om_out[:, sl] = (o / jnp.sum(e, axis=-1, keepdims=True)).astype(BF16)
        elif col < 3 * NA_WIDTH + MEM_WIDTH + CONV_CH:
            u_scr[...] = z
        else:
            glu_out[...] = u_scr[...] * jax.nn.sigmoid(z)


def _in_proj(x, layer, seq_len, km, vm, norm_mix_g, w_in, na_q_norm_g, na_k_norm_g, mem_q_norm_g):
    n = x.shape[0]
    tm = TOKEN_TILE
    tiles_per_seq = seq_len // tm
    vec = lambda width: pl.BlockSpec((None, 1, width), lambda i: (layer, 0, 0))
    head_major = pl.BlockSpec((NA_HEADS, tm, HEAD_DIM), lambda i: (0, i, 0))
    mem_spec = pl.BlockSpec((None, N_MEM, MEM_WIDTH), lambda i: (i // tiles_per_seq, 0, 0))
    return pl.pallas_call(
        _in_proj_kernel,
        grid=(n // tm,),
        in_specs=[
            pl.BlockSpec((tm, D_MODEL), lambda i: (i, 0)),
            vec(D_MODEL),
            pl.BlockSpec((None, D_MODEL, IN_WIDTH), lambda i: (layer, 0, 0), pipeline_mode=pl.Buffered(1)),
            vec(HEAD_DIM), vec(HEAD_DIM), vec(HEAD_DIM),
            mem_spec, mem_spec,
        ],
        out_specs=[
            head_major, head_major, head_major,
            pl.BlockSpec((tm, MEM_WIDTH), lambda i: (i, 0)),
            pl.BlockSpec((tm, CONV_CH), lambda i: (i, 0)),
        ],
        out_shape=[
            jax.ShapeDtypeStruct((NA_HEADS, n, HEAD_DIM), BF16),
            jax.ShapeDtypeStruct((NA_HEADS, n, HEAD_DIM), BF16),
            jax.ShapeDtypeStruct((NA_HEADS, n, HEAD_DIM), BF16),
            jax.ShapeDtypeStruct((n, MEM_WIDTH), BF16),
            jax.ShapeDtypeStruct((n, CONV_CH), F32),
        ],
        scratch_shapes=[pltpu.VMEM((tm, D_MODEL), BF16), pltpu.VMEM((tm, CONV_CH), F32)],
        compiler_params=pltpu.CompilerParams(vmem_limit_bytes=VMEM_LIMIT),
        name="in_proj",
    )(x, norm_mix_g, w_in, na_q_norm_g, na_k_norm_g, mem_q_norm_g, km, vm)


def _na_kernel(q_ref, k_ref, v_ref, bias_ref, o_ref):
    rows = q_ref.shape[0] // GRID_W

    def group_step(g, carry):
        row_ids = [g * NA_ROWS_PER_STEP + u for u in range(NA_ROWS_PER_STEP)]
        starts = [jnp.clip(r - NA_WIN_ROWS // 2, 0, rows - NA_WIN_ROWS) for r in row_ids]
        q0s = [pl.multiple_of(r * GRID_W, GRID_W) for r in row_ids]
        k0s = [pl.multiple_of(rs * GRID_W, GRID_W) for rs in starts]
        scores = []
        for q0, k0 in zip(q0s, k0s):
            q = q_ref[pl.ds(q0, GRID_W), :]
            kw = k_ref[pl.ds(k0, WIN_KEYS), :]
            scores.append(lax.dot_general(q, kw, (((1,), (1,)), ((), ())), preferred_element_type=F32))
        probs, sums = [], []
        for r, rs, s in zip(row_ids, starts, scores):
            s = s * ATTN_SCALE + bias_ref[r - rs]
            e = jnp.exp(s - jnp.max(s, axis=-1, keepdims=True))
            probs.append(e.astype(BF16))
            sums.append(jnp.sum(e, axis=-1, keepdims=True))
        for q0, k0, p, l in zip(q0s, k0s, probs, sums):
            o = jnp.dot(p, v_ref[pl.ds(k0, WIN_KEYS), :], preferred_element_type=F32)
            o_ref[pl.ds(q0, GRID_W), :] = (o / l).astype(BF16)
        return carry

    lax.fori_loop(0, rows // NA_ROWS_PER_STEP, group_step, 0)


def _na_attention(q, k, v, bias, layer, seq_len):
    n = q.shape[1]
    seq = pl.BlockSpec((None, seq_len, HEAD_DIM), lambda b, h: (h, b, 0))
    return pl.pallas_call(
        _na_kernel,
        grid=(n // seq_len, NA_HEADS),
        in_specs=[seq, seq, seq,
                  pl.BlockSpec((None, None, NA_WIN_ROWS, GRID_W, WIN_KEYS), lambda b, h: (layer, h, 0, 0, 0))],
        out_specs=seq,
        out_shape=jax.ShapeDtypeStruct((NA_HEADS, n, HEAD_DIM), BF16),
        compiler_params=pltpu.CompilerParams(vmem_limit_bytes=VMEM_LIMIT),
        name="na_attn",
    )(q, k, v, bias)


def _out_proj_kernel(x_ref, ona_ref, om_ref, glu_prev, glu_cur, glu_next, cw_ref, cb_ref, lg_ref, lb_ref,
                     w_ref, y_ref, ext_scr, shift_scr, cat_scr, *, tiles_per_seq):
    tm = x_ref.shape[0]
    attn_width = NA_WIDTH + MEM_WIDTH
    pos = pl.program_id(0) % tiles_per_seq
    ext_scr[0:HALO, :] = jnp.where(pos == 0, 0.0, glu_prev[...])
    ext_scr[HALO:HALO + tm, :] = glu_cur[...]
    ext_scr[HALO + tm:, :] = jnp.where(pos == tiles_per_seq - 1, 0.0, glu_next[...])
    for hd in range(NA_HEADS):
        cat_scr[:, hd * HEAD_DIM:(hd + 1) * HEAD_DIM] = ona_ref[hd]
    cat_scr[:, NA_WIDTH:attn_width] = om_ref[...]

    for c in range(D_MODEL // IN_CHUNK):
        sl = slice(c * IN_CHUNK, (c + 1) * IN_CHUNK)
        y_ref[:, sl] = x_ref[:, sl] + jnp.dot(cat_scr[:, :attn_width], w_ref[:attn_width, sl],
                                              preferred_element_type=F32)

    lead = HALO - (CONV_K - 1) // 2
    span = shift_scr.shape[1]
    for b in range(1, SUBLANES):
        shift_scr[b - 1] = ext_scr[b:b + span, :]
    rep = lambda v: jnp.concatenate([v] * (CONV_ROWS // SUBLANES), axis=0)
    for r0 in range(0, tm, CONV_ROWS):
        acc = rep(cb_ref[...])
        for tap in range(CONV_K):
            a, b = divmod(tap + lead, SUBLANES)
            off = r0 + SUBLANES * a
            win = ext_scr[off:off + CONV_ROWS, :] if b == 0 else shift_scr[b - 1, off:off + CONV_ROWS, :]
            acc = acc + win * rep(cw_ref[tap])
        mu = jnp.mean(acc, axis=-1, keepdims=True)
        cen = acc - mu
        var = jnp.mean(cen * cen, axis=-1, keepdims=True)
        yln = (cen * lax.rsqrt(var + EPS)) * rep(lg_ref[...]) + rep(lb_ref[...])
        cat_scr[r0:r0 + CONV_ROWS, attn_width:] = (yln * jax.nn.sigmoid(yln)).astype(BF16)

    for c in range(D_MODEL // IN_CHUNK):
        sl = slice(c * IN_CHUNK, (c + 1) * IN_CHUNK)
        y_ref[:, sl] += jnp.dot(cat_scr[:, attn_width:], w_ref[attn_width:, sl], preferred_element_type=F32)


def _out_proj(x, ona, om, glu, layer, seq_len, conv_dw_w, conv_dw_b, conv_ln_g, conv_ln_b, w_out):
    n = x.shape[0]
    tm = TOKEN_TILE
    tiles_per_seq = seq_len // tm
    halo_per_tile = tm // HALO
    last_halo = n // HALO - 1
    sub_vec = pl.BlockSpec((None, SUBLANES, CONV_CH), lambda i: (layer, 0, 0))
    return pl.pallas_call(
        functools.partial(_out_proj_kernel, tiles_per_seq=tiles_per_seq),
        grid=(n // tm,),
        in_specs=[
            pl.BlockSpec((tm, D_MODEL), lambda i: (i, 0)),
            pl.BlockSpec((NA_HEADS, tm, HEAD_DIM), lambda i: (0, i, 0)),
            pl.BlockSpec((tm, MEM_WIDTH), lambda i: (i, 0)),
            pl.BlockSpec((HALO, CONV_CH), lambda i: (jnp.maximum(i * halo_per_tile - 1, 0), 0)),
            pl.BlockSpec((tm, CONV_CH), lambda i: (i, 0)),
            pl.BlockSpec((HALO, CONV_CH), lambda i: (jnp.minimum((i + 1) * halo_per_tile, last_halo), 0)),
            pl.BlockSpec((None, CONV_K, SUBLANES, CONV_CH), lambda i: (layer, 0, 0, 0)),
            sub_vec, sub_vec, sub_vec,
            pl.BlockSpec((None, D_MODEL, D_MODEL), lambda i: (layer, 0, 0), pipeline_mode=pl.Buffered(1)),
        ],
        out_specs=pl.BlockSpec((tm, D_MODEL), lambda i: (i, 0)),
        out_shape=jax.ShapeDtypeStruct((n, D_MODEL), F32),
        scratch_shapes=[
            pltpu.VMEM((tm + 2 * HALO, CONV_CH), F32),
            pltpu.VMEM((SUBLANES - 1, tm + 2 * HALO - SUBLANES, CONV_CH), F32),
            pltpu.VMEM((tm, D_MODEL), BF16),
        ],
        compiler_params=pltpu.CompilerParams(vmem_limit_bytes=VMEM_LIMIT),
        name="out_proj",
    )(x, ona, om, glu, glu, glu, conv_dw_w, conv_dw_b, conv_ln_g, conv_ln_b, w_out)


def _ffn_kernel(x_prev, x_cur, x_next, ng_ref, wup_ref, cw_ref, cb_ref, wd_ref, y_ref,
                h_scr, z_scr, act_scr, *, tiles_per_seq):
    tm = x_cur.shape[0]
    j = pl.program_id(1)

    @pl.when(j == 0)
    def _():
        pos = pl.program_id(0) % tiles_per_seq
        hp = _rms_rows(x_prev[...], ng_ref[...])
        h_scr[0:HALO, :] = jnp.where(pos == 0, 0.0, hp).astype(BF16)
        hn = _rms_rows(x_next[...], ng_ref[...])
        h_scr[HALO + tm:, :] = jnp.where(pos == tiles_per_seq - 1, 0.0, hn).astype(BF16)

        def norm_chunk(c, carry):
            r0 = pl.multiple_of(c * ROW_CHUNK, ROW_CHUNK)
            xc = x_cur[pl.ds(r0, ROW_CHUNK), :]
            h_scr[pl.ds(HALO + r0, ROW_CHUNK), :] = _rms_rows(xc, ng_ref[...]).astype(BF16)
            y_ref[pl.ds(r0, ROW_CHUNK), :] = xc
            return carry

        lax.fori_loop(0, tm // ROW_CHUNK, norm_chunk, 0, unroll=NORM_UNROLL)

    z_scr[...] = jnp.dot(h_scr[...], wup_ref[...], preferred_element_type=F32)

    rep = lambda v: jnp.concatenate([v] * (ROW_CHUNK // SUBLANES), axis=0)

    def gated_rows(r0):
        acc = rep(cb_ref[...])
        for tap in range(3):
            off = r0 + HALO - 1 + tap
            acc = acc + z_scr[off:off + ROW_CHUNK, :] * rep(cw_ref[tap])
        gate, val = acc[:, :FF_BLOCK], acc[:, FF_BLOCK:]
        return (gate * jax.nn.sigmoid(gate)) * val

    rows = tm // FFN_ROW_GROUPS
    for lo in range(0, tm, rows):
        for r0 in range(lo, lo + rows, ROW_CHUNK):
            act_scr[r0:r0 + ROW_CHUNK, :] = gated_rows(r0).astype(BF16)
        y_ref[lo:lo + rows, :] += jnp.dot(act_scr[lo:lo + rows, :], wd_ref[...], preferred_element_type=F32)


def _ffn(x, layer, seq_len, norm_ffn_g, w_up_blocks, ffn_cw_blocks, ffn_cb_blocks, w_down):
    n = x.shape[0]
    tm = FFN_TOKEN_TILE
    tiles_per_seq = seq_len // tm
    halo_per_tile = tm // HALO
    last_halo = n // HALO - 1
    nblk = D_FF // FF_BLOCK
    return pl.pallas_call(
        functools.partial(_ffn_kernel, tiles_per_seq=tiles_per_seq),
        grid=(n // tm, nblk),
        in_specs=[
            pl.BlockSpec((HALO, D_MODEL), lambda i, j: (jnp.maximum(i * halo_per_tile - 1, 0), 0)),
            pl.BlockSpec((tm, D_MODEL), lambda i, j: (i, 0)),
            pl.BlockSpec((HALO, D_MODEL), lambda i, j: (jnp.minimum((i + 1) * halo_per_tile, last_halo), 0)),
            pl.BlockSpec((None, 1, D_MODEL), lambda i, j: (layer, 0, 0)),
            pl.BlockSpec((None, None, D_MODEL, 2 * FF_BLOCK), lambda i, j: (layer, j, 0, 0)),
            pl.BlockSpec((None, None, 3, SUBLANES, 2 * FF_BLOCK), lambda i, j: (layer, j, 0, 0, 0)),
            pl.BlockSpec((None, None, SUBLANES, 2 * FF_BLOCK), lambda i, j: (layer, j, 0, 0)),
            pl.BlockSpec((None, FF_BLOCK, D_MODEL), lambda i, j: (layer, j, 0)),
        ],
        out_specs=pl.BlockSpec((tm, D_MODEL), lambda i, j: (i, 0)),
        out_shape=jax.ShapeDtypeStruct((n, D_MODEL), F32),
        scratch_shapes=[
            pltpu.VMEM((tm + 2 * HALO, D_MODEL), BF16),
            pltpu.VMEM((tm + 2 * HALO, 2 * FF_BLOCK), F32),
            pltpu.VMEM((tm, FF_BLOCK), BF16),
        ],
        compiler_params=pltpu.CompilerParams(
            dimension_semantics=("arbitrary", "arbitrary"), vmem_limit_bytes=VMEM_LIMIT),
        name="ffn",
    )(x, x, x, norm_ffn_g, w_up_blocks, ffn_cw_blocks, ffn_cb_blocks, w_down)


def kernel(x_prompt, x_sample, mem_prompt, mem_sample, norm_mix_g, w_in, na_q_norm_g, na_k_norm_g, na_rpb,
           mem_norm_g, w_mem_kv, mem_q_norm_g, mem_k_norm_g, conv_dw_w, conv_dw_b, conv_ln_g, conv_ln_b, w_out,
           norm_ffn_g, w_up, ffn_dw_w, ffn_dw_b, w_down):
    depth = w_in.shape[0]
    row = lambda a: a.reshape(depth, 1, a.shape[-1])
    w_in_b, w_kv_b, w_out_b, w_down_b = (w.astype(BF16) for w in (w_in, w_mem_kv, w_out, w_down))
    norm_mix_g, na_q_norm_g, na_k_norm_g, mem_norm_g, mem_q_norm_g, mem_k_norm_g, norm_ffn_g = map(
        row, (norm_mix_g, na_q_norm_g, na_k_norm_g, mem_norm_g, mem_q_norm_g, mem_k_norm_g, norm_ffn_g))
    sub = lambda a: jnp.broadcast_to(a[..., None, :], a.shape[:-1] + (SUBLANES, a.shape[-1]))
    conv_dw_w, conv_dw_b, conv_ln_g, conv_ln_b = map(sub, (conv_dw_w, conv_dw_b, conv_ln_g, conv_ln_b))

    nblk = D_FF // FF_BLOCK

    def ff_blocks(a):
        lead = a.shape[:-1]
        a = a.reshape(lead + (2, nblk, FF_BLOCK))
        return jnp.moveaxis(a, -3, -2).reshape(lead + (nblk, 2 * FF_BLOCK))

    w_up_blocks = jnp.moveaxis(ff_blocks(w_up.astype(BF16)), -2, 1)
    ffn_cw_blocks = sub(jnp.moveaxis(ff_blocks(ffn_dw_w), -2, 1))
    ffn_cb_blocks = sub(ff_blocks(ffn_dw_b))
    bias = _bias_tables(na_rpb)

    def trunk(x3, mem):
        b, seq_len, _ = x3.shape
        x = x3.reshape(b * seq_len, D_MODEL)
        for layer in range(depth):
            km, vm = _mem_kv(mem, layer, mem_norm_g, w_kv_b, mem_k_norm_g)
            q, k, v, om, glu = _in_proj(x, layer, seq_len, km, vm, norm_mix_g, w_in_b,
                                        na_q_norm_g, na_k_norm_g, mem_q_norm_g)
            ona = _na_attention(q, k, v, bias, layer, seq_len)
            x = _out_proj(x, ona, om, glu, layer, seq_len, conv_dw_w, conv_dw_b, conv_ln_g, conv_ln_b, w_out_b)
            x = _ffn(x, layer, seq_len, norm_ffn_g, w_up_blocks, ffn_cw_blocks, ffn_cb_blocks, w_down_b)
        return x.reshape(b, seq_len, D_MODEL)

    return trunk(x_prompt, mem_prompt), trunk(x_sample, mem_sample)
```

```python
import functools

import jax
import jax.numpy as jnp
from jax import lax
from jax.experimental import pallas as pl
from jax.experimental.pallas import tpu as pltpu

F32 = jnp.float32
BF16 = jnp.bfloat16

D_MODEL = 2048
GRID_W = 64
HEAD_DIM = 128
NA_HEADS = 8
NA_WIDTH = NA_HEADS * HEAD_DIM
NA_WIN_ROWS = 8
NA_WIN_COLS = 16
RPB_ROWS = 2 * NA_WIN_ROWS - 1
RPB_COLS = 2 * NA_WIN_COLS - 1
MEM_HEADS = 4
MEM_WIDTH = MEM_HEADS * HEAD_DIM
N_MEM = 256
CONV_CH = D_MODEL - NA_WIDTH - MEM_WIDTH
CONV_K = 31
D_FF = 5632
IN_WIDTH = 3 * NA_WIDTH + MEM_WIDTH + 2 * CONV_CH
EPS = 1e-6
ATTN_SCALE = HEAD_DIM ** -0.5
MASK_VALUE = -1e30

WIN_KEYS = NA_WIN_ROWS * GRID_W
TOKEN_TILE = 512
FFN_TOKEN_TILE = 1024
HALO = 16
IN_CHUNK = 512
FF_BLOCK = 512
ROW_CHUNK = 32
NORM_UNROLL = 4
NA_ROWS_PER_STEP = 16
SUBLANES = 8
FFN_ROW_GROUPS = 2
CONV_ROWS = 16
VMEM_LIMIT = 60 * 1024 * 1024


def _rms_rows(x, g):
    ms = jnp.mean(x * x, axis=-1, keepdims=True)
    return (x * lax.rsqrt(ms + EPS)) * g


def _bias_kernel(rpb_ref, out_ref):
    l = pl.program_id(0)
    h = pl.program_id(1)
    oi = pl.program_id(2)
    shape = (GRID_W, WIN_KEYS)
    q = lax.broadcasted_iota(jnp.int32, shape, 0)
    lane = lax.broadcasted_iota(jnp.int32, shape, 1)
    kc = lane & (GRID_W - 1)
    d = kc - q + (NA_WIN_COLS - 1)
    cs = jnp.clip(q - NA_WIN_COLS // 2, 0, GRID_W - NA_WIN_COLS)
    valid = (kc >= cs) & (kc < cs + NA_WIN_COLS)
    lane_w = lax.broadcasted_iota(jnp.int32, (1, WIN_KEYS), 1) >> 6
    base = (l * NA_HEADS + h) * (RPB_ROWS * RPB_COLS)
    t = jnp.full(shape, MASK_VALUE, F32)
    for dd in range(RPB_COLS):
        row = jnp.zeros((1, WIN_KEYS), F32)
        for w in range(NA_WIN_ROWS):
            val = rpb_ref[base + (w - oi + NA_WIN_ROWS - 1) * RPB_COLS + dd]
            row = jnp.where(lane_w == w, val, row)
        t = jnp.where(valid & (d == dd), jnp.broadcast_to(row, shape), t)
    out_ref[...] = t


def _bias_tables(na_rpb):
    depth = na_rpb.shape[0]
    return pl.pallas_call(
        _bias_kernel,
        grid=(depth, NA_HEADS, NA_WIN_ROWS),
        in_specs=[pl.BlockSpec(memory_space=pltpu.SMEM)],
        out_specs=pl.BlockSpec((None, None, None, GRID_W, WIN_KEYS), lambda l, h, o: (l, h, o, 0, 0)),
        out_shape=jax.ShapeDtypeStruct((depth, NA_HEADS, NA_WIN_ROWS, GRID_W, WIN_KEYS), F32),
        name="rpb_tables",
    )(na_rpb.reshape(-1))


def _mem_kv_kernel(mem_ref, g_ref, w_ref, kg_ref, k_out, v_out):
    h = _rms_rows(mem_ref[...], g_ref[...]).astype(BF16)
    kv = jnp.dot(h, w_ref[...], preferred_element_type=F32)
    for hd in range(MEM_HEADS):
        sl = slice(hd * HEAD_DIM, (hd + 1) * HEAD_DIM)
        k_out[:, sl] = _rms_rows(kv[:, sl], kg_ref[...]).astype(BF16)
    v_out[...] = kv[:, MEM_WIDTH:].astype(BF16)


def _mem_kv(mem, layer, mem_norm_g, w_mem_kv, mem_k_norm_g):
    b = mem.shape[0]
    return pl.pallas_call(
        _mem_kv_kernel,
        grid=(b,),
        in_specs=[
            pl.BlockSpec((None, N_MEM, D_MODEL), lambda i: (i, 0, 0)),
            pl.BlockSpec((None, 1, D_MODEL), lambda i: (layer, 0, 0)),
            pl.BlockSpec((None, D_MODEL, 2 * MEM_WIDTH), lambda i: (layer, 0, 0)),
            pl.BlockSpec((None, 1, HEAD_DIM), lambda i: (layer, 0, 0)),
        ],
        out_specs=[
            pl.BlockSpec((None, N_MEM, MEM_WIDTH), lambda i: (i, 0, 0)),
            pl.BlockSpec((None, N_MEM, MEM_WIDTH), lambda i: (i, 0, 0)),
        ],
        out_shape=[jax.ShapeDtypeStruct((b, N_MEM, MEM_WIDTH), BF16)] * 2,
        compiler_params=pltpu.CompilerParams(vmem_limit_bytes=VMEM_LIMIT),
        name="mem_kv",
    )(mem, mem_norm_g, w_mem_kv, mem_k_norm_g)


def _in_proj_kernel(x_ref, ng_ref, w_ref, qg_ref, kg_ref, mqg_ref, km_ref, vm_ref,
                    q_out, k_out, v_out, om_out, glu_out, h_scr, u_scr):
    tm = x_ref.shape[0]

    def norm_chunk(c, carry):
        r0 = pl.multiple_of(c * ROW_CHUNK, ROW_CHUNK)
        h_scr[pl.ds(r0, ROW_CHUNK), :] = _rms_rows(x_ref[pl.ds(r0, ROW_CHUNK), :], ng_ref[...]).astype(BF16)
        return carry

    lax.fori_loop(0, tm // ROW_CHUNK, norm_chunk, 0, unroll=NORM_UNROLL)

    heads_per_chunk = IN_CHUNK // HEAD_DIM
    for c in range(IN_WIDTH // IN_CHUNK):
        z = jnp.dot(h_scr[...], w_ref[:, c * IN_CHUNK:(c + 1) * IN_CHUNK], preferred_element_type=F32)
        col = c * IN_CHUNK
        if col < 2 * NA_WIDTH:
            is_q = col < NA_WIDTH
            out, g_ref = (q_out, qg_ref) if is_q else (k_out, kg_ref)
            head0 = (col - (0 if is_q else NA_WIDTH)) // HEAD_DIM
            for j in range(heads_per_chunk):
                zh = z[:, j * HEAD_DIM:(j + 1) * HEAD_DIM]
                out[head0 + j] = _rms_rows(zh, g_ref[...]).astype(BF16)
        elif col < 3 * NA_WIDTH:
            head0 = (col - 2 * NA_WIDTH) // HEAD_DIM
            for j in range(heads_per_chunk):
                v_out[head0 + j] = z[:, j * HEAD_DIM:(j + 1) * HEAD_DIM].astype(BF16)
        elif col < 3 * NA_WIDTH + MEM_WIDTH:
            for j in range(MEM_HEADS):
                sl = slice(j * HEAD_DIM, (j + 1) * HEAD_DIM)
                qh = _rms_rows(z[:, sl], mqg_ref[...]).astype(BF16)
                s = lax.dot_general(qh, km_ref[:, sl], (((1,), (1,)), ((), ())),
                                    preferred_element_type=F32) * ATTN_SCALE
                e = jnp.exp(s - jnp.max(s, axis=-1, keepdims=True))
                o = jnp.dot(e.astype(BF16), vm_ref[:, sl], preferred_element_type=F32)
                om_out[:, sl] = (o / jnp.sum(e, axis=-1, keepdims=True)).astype(BF16)
        elif col < 3 * NA_WIDTH + MEM_WIDTH + CONV_CH:
            u_scr[...] = z
        else:
            glu_out[...] = u_scr[...] * jax.nn.sigmoid(z)


def _in_proj(x, layer, seq_len, km, vm, norm_mix_g, w_in, na_q_norm_g, na_k_norm_g, mem_q_norm_g):
    n = x.shape[0]
    tm = TOKEN_TILE
    tiles_per_seq = seq_len // tm
    vec = lambda width: pl.BlockSpec((None, 1, width), lambda i: (layer, 0, 0))
    head_major = pl.BlockSpec((NA_HEADS, tm, HEAD_DIM), lambda i: (0, i, 0))
    mem_spec = pl.BlockSpec((None, N_MEM, MEM_WIDTH), lambda i: (i // tiles_per_seq, 0, 0))
    return pl.pallas_call(
        _in_proj_kernel,
        grid=(n // tm,),
        in_specs=[
            pl.BlockSpec((tm, D_MODEL), lambda i: (i, 0)),
            vec(D_MODEL),
            pl.BlockSpec((None, D_MODEL, IN_WIDTH), lambda i: (layer, 0, 0), pipeline_mode=pl.Buffered(1)),
            vec(HEAD_DIM), vec(HEAD_DIM), vec(HEAD_DIM),
            mem_spec, mem_spec,
        ],
        out_specs=[
            head_major, head_major, head_major,
            pl.BlockSpec((tm, MEM_WIDTH), lambda i: (i, 0)),
            pl.BlockSpec((tm, CONV_CH), lambda i: (i, 0)),
        ],
        out_shape=[
            jax.ShapeDtypeStruct((NA_HEADS, n, HEAD_DIM), BF16),
            jax.ShapeDtypeStruct((NA_HEADS, n, HEAD_DIM), BF16),
            jax.ShapeDtypeStruct((NA_HEADS, n, HEAD_DIM), BF16),
            jax.ShapeDtypeStruct((n, MEM_WIDTH), BF16),
            jax.ShapeDtypeStruct((n, CONV_CH), F32),
        ],
        scratch_shapes=[pltpu.VMEM((tm, D_MODEL), BF16), pltpu.VMEM((tm, CONV_CH), F32)],
        compiler_params=pltpu.CompilerParams(vmem_limit_bytes=VMEM_LIMIT),
        name="in_proj",
    )(x, norm_mix_g, w_in, na_q_norm_g, na_k_norm_g, mem_q_norm_g, km, vm)


def _na_kernel(q_ref, k_ref, v_ref, bias_ref, o_ref):
    rows = q_ref.shape[0] // GRID_W

    def group_step(g, carry):
        row_ids = [g * NA_ROWS_PER_STEP + u for u in range(NA_ROWS_PER_STEP)]
        starts = [jnp.clip(r - NA_WIN_ROWS // 2, 0, rows - NA_WIN_ROWS) for r in row_ids]
        q0s = [pl.multiple_of(r * GRID_W, GRID_W) for r in row_ids]
        k0s = [pl.multiple_of(rs * GRID_W, GRID_W) for rs in starts]
        scores = []
        for q0, k0 in zip(q0s, k0s):
            q = q_ref[pl.ds(q0, GRID_W), :]
            kw = k_ref[pl.ds(k0, WIN_KEYS), :]
            scores.append(lax.dot_general(q, kw, (((1,), (1,)), ((), ())), preferred_element_type=F32))
        probs, sums = [], []
        for r, rs, s in zip(row_ids, starts, scores):
            s = s * ATTN_SCALE + bias_ref[r - rs]
            e = jnp.exp(s - jnp.max(s, axis=-1, keepdims=True))
            probs.append(e.astype(BF16))
            sums.append(jnp.sum(e, axis=-1, keepdims=True))
        for q0, k0, p, l in zip(q0s, k0s, probs, sums):
            o = jnp.dot(p, v_ref[pl.ds(k0, WIN_KEYS), :], preferred_element_type=F32)
            o_ref[pl.ds(q0, GRID_W), :] = (o / l).astype(BF16)
        return carry

    lax.fori_loop(0, rows // NA_ROWS_PER_STEP, group_step, 0)


def _na_attention(q, k, v, bias, layer, seq_len):
    n = q.shape[1]
    seq = pl.BlockSpec((None, seq_len, HEAD_DIM), lambda b, h: (h, b, 0))
    return pl.pallas_call(
        _na_kernel,
        grid=(n // seq_len, NA_HEADS),
        in_specs=[seq, seq, seq,
                  pl.BlockSpec((None, None, NA_WIN_ROWS, GRID_W, WIN_KEYS), lambda b, h: (layer, h, 0, 0, 0))],
        out_specs=seq,
        out_shape=jax.ShapeDtypeStruct((NA_HEADS, n, HEAD_DIM), BF16),
        compiler_params=pltpu.CompilerParams(vmem_limit_bytes=VMEM_LIMIT),
        name="na_attn",
    )(q, k, v, bias)


def _out_proj_kernel(x_ref, ona_ref, om_ref, glu_prev, glu_cur, glu_next, cw_ref, cb_ref, lg_ref, lb_ref,
                     w_ref, y_ref, ext_scr, shift_scr, cat_scr, *, tiles_per_seq):
    tm = x_ref.shape[0]
    attn_width = NA_WIDTH + MEM_WIDTH
    pos = pl.program_id(0) % tiles_per_seq
    ext_scr[0:HALO, :] = jnp.where(pos == 0, 0.0, glu_prev[...])
    ext_scr[HALO:HALO + tm, :] = glu_cur[...]
    ext_scr[HALO + tm:, :] = jnp.where(pos == tiles_per_seq - 1, 0.0, glu_next[...])
    for hd in range(NA_HEADS):
        cat_scr[:, hd * HEAD_DIM:(hd + 1) * HEAD_DIM] = ona_ref[hd]
    cat_scr[:, NA_WIDTH:attn_width] = om_ref[...]

    for c in range(D_MODEL // IN_CHUNK):
        sl = slice(c * IN_CHUNK, (c + 1) * IN_CHUNK)
        y_ref[:, sl] = x_ref[:, sl] + jnp.dot(cat_scr[:, :attn_width], w_ref[:attn_width, sl],
                                              preferred_element_type=F32)

    lead = HALO - (CONV_K - 1) // 2
    span = shift_scr.shape[1]
    for b in range(1, SUBLANES):
        shift_scr[b - 1] = ext_scr[b:b + span, :]
    rep = lambda v: jnp.concatenate([v] * (CONV_ROWS // SUBLANES), axis=0)
    for r0 in range(0, tm, CONV_ROWS):
        acc = rep(cb_ref[...])
        for tap in range(CONV_K):
            a, b = divmod(tap + lead, SUBLANES)
            off = r0 + SUBLANES * a
            win = ext_scr[off:off + CONV_ROWS, :] if b == 0 else shift_scr[b - 1, off:off + CONV_ROWS, :]
            acc = acc + win * rep(cw_ref[tap])
        mu = jnp.mean(acc, axis=-1, keepdims=True)
        cen = acc - mu
        var = jnp.mean(cen * cen, axis=-1, keepdims=True)
        yln = (cen * lax.rsqrt(var + EPS)) * rep(lg_ref[...]) + rep(lb_ref[...])
        cat_scr[r0:r0 + CONV_ROWS, attn_width:] = (yln * jax.nn.sigmoid(yln)).astype(BF16)

    for c in range(D_MODEL // IN_CHUNK):
        sl = slice(c * IN_CHUNK, (c + 1) * IN_CHUNK)
        y_ref[:, sl] += jnp.dot(cat_scr[:, attn_width:], w_ref[attn_width:, sl], preferred_element_type=F32)


def _out_proj(x, ona, om, glu, layer, seq_len, conv_dw_w, conv_dw_b, conv_ln_g, conv_ln_b, w_out):
    n = x.shape[0]
    tm = TOKEN_TILE
    tiles_per_seq = seq_len // tm
    halo_per_tile = tm // HALO
    last_halo = n // HALO - 1
    sub_vec = pl.BlockSpec((None, SUBLANES, CONV_CH), lambda i: (layer, 0, 0))
    return pl.pallas_call(
        functools.partial(_out_proj_kernel, tiles_per_seq=tiles_per_seq),
        grid=(n // tm,),
        in_specs=[
            pl.BlockSpec((tm, D_MODEL), lambda i: (i, 0)),
            pl.BlockSpec((NA_HEADS, tm, HEAD_DIM), lambda i: (0, i, 0)),
            pl.BlockSpec((tm, MEM_WIDTH), lambda i: (i, 0)),
            pl.BlockSpec((HALO, CONV_CH), lambda i: (jnp.maximum(i * halo_per_tile - 1, 0), 0)),
            pl.BlockSpec((tm, CONV_CH), lambda i: (i, 0)),
            pl.BlockSpec((HALO, CONV_CH), lambda i: (jnp.minimum((i + 1) * halo_per_tile, last_halo), 0)),
            pl.BlockSpec((None, CONV_K, SUBLANES, CONV_CH), lambda i: (layer, 0, 0, 0)),
            sub_vec, sub_vec, sub_vec,
            pl.BlockSpec((None, D_MODEL, D_MODEL), lambda i: (layer, 0, 0), pipeline_mode=pl.Buffered(1)),
        ],
        out_specs=pl.BlockSpec((tm, D_MODEL), lambda i: (i, 0)),
        out_shape=jax.ShapeDtypeStruct((n, D_MODEL), F32),
        scratch_shapes=[
            pltpu.VMEM((tm + 2 * HALO, CONV_CH), F32),
            pltpu.VMEM((SUBLANES - 1, tm + 2 * HALO - SUBLANES, CONV_CH), F32),
            pltpu.VMEM((tm, D_MODEL), BF16),
        ],
        compiler_params=pltpu.CompilerParams(vmem_limit_bytes=VMEM_LIMIT),
        name="out_proj",
    )(x, ona, om, glu, glu, glu, conv_dw_w, conv_dw_b, conv_ln_g, conv_ln_b, w_out)


def _ffn_kernel(x_prev, x_cur, x_next, ng_ref, wg_ref, wv_ref, cw_ref, cb_ref, wd_ref, y_ref,
                h_scr, z_scr, act_scr, *, tiles_per_seq):
    tm = x_cur.shape[0]
    j = pl.program_id(1)

    @pl.when(j == 0)
    def _():
        pos = pl.program_id(0) % tiles_per_seq
        hp = _rms_rows(x_prev[...], ng_ref[...])
        h_scr[0:HALO, :] = jnp.where(pos == 0, 0.0, hp).astype(BF16)
        hn = _rms_rows(x_next[...], ng_ref[...])
        h_scr[HALO + tm:, :] = jnp.where(pos == tiles_per_seq - 1, 0.0, hn).astype(BF16)

        def norm_chunk(c, carry):
            r0 = pl.multiple_of(c * ROW_CHUNK, ROW_CHUNK)
            xc = x_cur[pl.ds(r0, ROW_CHUNK), :]
            h_scr[pl.ds(HALO + r0, ROW_CHUNK), :] = _rms_rows(xc, ng_ref[...]).astype(BF16)
            y_ref[pl.ds(r0, ROW_CHUNK), :] = xc
            return carry

        lax.fori_loop(0, tm // ROW_CHUNK, norm_chunk, 0, unroll=NORM_UNROLL)

    z_scr[:, :FF_BLOCK] = jnp.dot(h_scr[...], wg_ref[...], preferred_element_type=F32)
    z_scr[:, FF_BLOCK:] = jnp.dot(h_scr[...], wv_ref[...], preferred_element_type=F32)

    rep = lambda v: jnp.concatenate([v] * (ROW_CHUNK // SUBLANES), axis=0)

    def gated_rows(r0):
        acc = rep(cb_ref[...])
        for tap in range(3):
            off = r0 + HALO - 1 + tap
            acc = acc + z_scr[off:off + ROW_CHUNK, :] * rep(cw_ref[tap])
        gate, val = acc[:, :FF_BLOCK], acc[:, FF_BLOCK:]
        return (gate * jax.nn.sigmoid(gate)) * val

    rows = tm // FFN_ROW_GROUPS
    for lo in range(0, tm, rows):
        for r0 in range(lo, lo + rows, ROW_CHUNK):
            act_scr[r0:r0 + ROW_CHUNK, :] = gated_rows(r0).astype(BF16)
        y_ref[lo:lo + rows, :] += jnp.dot(act_scr[lo:lo + rows, :], wd_ref[...], preferred_element_type=F32)


def _ffn(x, layer, seq_len, norm_ffn_g, w_up, ffn_cw_blocks, ffn_cb_blocks, w_down):
    n = x.shape[0]
    tm = FFN_TOKEN_TILE
    tiles_per_seq = seq_len // tm
    halo_per_tile = tm // HALO
    last_halo = n // HALO - 1
    nblk = D_FF // FF_BLOCK
    return pl.pallas_call(
        functools.partial(_ffn_kernel, tiles_per_seq=tiles_per_seq),
        grid=(n // tm, nblk),
        in_specs=[
            pl.BlockSpec((HALO, D_MODEL), lambda i, j: (jnp.maximum(i * halo_per_tile - 1, 0), 0)),
            pl.BlockSpec((tm, D_MODEL), lambda i, j: (i, 0)),
            pl.BlockSpec((HALO, D_MODEL), lambda i, j: (jnp.minimum((i + 1) * halo_per_tile, last_halo), 0)),
            pl.BlockSpec((None, 1, D_MODEL), lambda i, j: (layer, 0, 0)),
            pl.BlockSpec((None, D_MODEL, FF_BLOCK), lambda i, j: (layer, 0, j)),
            pl.BlockSpec((None, D_MODEL, FF_BLOCK), lambda i, j: (layer, 0, nblk + j)),
            pl.BlockSpec((None, None, 3, SUBLANES, 2 * FF_BLOCK), lambda i, j: (layer, j, 0, 0, 0)),
            pl.BlockSpec((None, None, SUBLANES, 2 * FF_BLOCK), lambda i, j: (layer, j, 0, 0)),
            pl.BlockSpec((None, FF_BLOCK, D_MODEL), lambda i, j: (layer, j, 0)),
        ],
        out_specs=pl.BlockSpec((tm, D_MODEL), lambda i, j: (i, 0)),
        out_shape=jax.ShapeDtypeStruct((n, D_MODEL), F32),
        scratch_shapes=[
            pltpu.VMEM((tm + 2 * HALO, D_MODEL), BF16),
            pltpu.VMEM((tm + 2 * HALO, 2 * FF_BLOCK), F32),
            pltpu.VMEM((tm, FF_BLOCK), BF16),
        ],
        compiler_params=pltpu.CompilerParams(
            dimension_semantics=("arbitrary", "arbitrary"), vmem_limit_bytes=VMEM_LIMIT),
        name="ffn",
    )(x, x, x, norm_ffn_g, w_up, w_up, ffn_cw_blocks, ffn_cb_blocks, w_down)


def kernel(x_prompt, x_sample, mem_prompt, mem_sample, norm_mix_g, w_in, na_q_norm_g, na_k_norm_g, na_rpb,
           mem_norm_g, w_mem_kv, mem_q_norm_g, mem_k_norm_g, conv_dw_w, conv_dw_b, conv_ln_g, conv_ln_b, w_out,
           norm_ffn_g, w_up, ffn_dw_w, ffn_dw_b, w_down):
    depth = w_in.shape[0]
    row = lambda a: a.reshape(depth, 1, a.shape[-1])
    w_in_b, w_kv_b, w_out_b, w_up_b, w_down_b = (w.astype(BF16) for w in (w_in, w_mem_kv, w_out, w_up, w_down))
    norm_mix_g, na_q_norm_g, na_k_norm_g, mem_norm_g, mem_q_norm_g, mem_k_norm_g, norm_ffn_g = map(
        row, (norm_mix_g, na_q_norm_g, na_k_norm_g, mem_norm_g, mem_q_norm_g, mem_k_norm_g, norm_ffn_g))
    sub = lambda a: jnp.broadcast_to(a[..., None, :], a.shape[:-1] + (SUBLANES, a.shape[-1]))
    conv_dw_w, conv_dw_b, conv_ln_g, conv_ln_b = map(sub, (conv_dw_w, conv_dw_b, conv_ln_g, conv_ln_b))

    nblk = D_FF // FF_BLOCK

    def ff_blocks(a):
        lead = a.shape[:-1]
        a = a.reshape(lead + (2, nblk, FF_BLOCK))
        return jnp.moveaxis(a, -3, -2).reshape(lead + (nblk, 2 * FF_BLOCK))

    ffn_cw_blocks = sub(jnp.moveaxis(ff_blocks(ffn_dw_w), -2, 1))
    ffn_cb_blocks = sub(ff_blocks(ffn_dw_b))
    bias = _bias_tables(na_rpb)

    def trunk(x3, mem):
        b, seq_len, _ = x3.shape
        x = x3.reshape(b * seq_len, D_MODEL)
        for layer in range(depth):
            km, vm = _mem_kv(mem, layer, mem_norm_g, w_kv_b, mem_k_norm_g)
            q, k, v, om, glu = _in_proj(x, layer, seq_len, km, vm, norm_mix_g, w_in_b,
                                        na_q_norm_g, na_k_norm_g, mem_q_norm_g)
            ona = _na_attention(q, k, v, bias, layer, seq_len)
            x = _out_proj(x, ona, om, glu, layer, seq_len, conv_dw_w, conv_dw_b, conv_ln_g, conv_ln_b, w_out_b)
            x = _ffn(x, layer, seq_len, norm_ffn_g, w_up_b, ffn_cw_blocks, ffn_cb_blocks, w_down_b)
        return x.reshape(b, seq_len, D_MODEL)

    return trunk(x_prompt, mem_prompt), trunk(x_sample, mem_sample)
```

```python
import functools

import jax
import jax.numpy as jnp
from jax import lax
from jax.experimental import pallas as pl
from jax.experimental.pallas import tpu as pltpu

F32 = jnp.float32
BF16 = jnp.bfloat16

D_MODEL = 2048
GRID_W = 64
HEAD_DIM = 128
NA_HEADS = 8
NA_WIDTH = NA_HEADS * HEAD_DIM
NA_WIN_ROWS = 8
NA_WIN_COLS = 16
RPB_ROWS = 2 * NA_WIN_ROWS - 1
RPB_COLS = 2 * NA_WIN_COLS - 1
MEM_HEADS = 4
MEM_WIDTH = MEM_HEADS * HEAD_DIM
N_MEM = 256
CONV_CH = D_MODEL - NA_WIDTH - MEM_WIDTH
CONV_K = 31
D_FF = 5632
IN_WIDTH = 3 * NA_WIDTH + MEM_WIDTH + 2 * CONV_CH
EPS = 1e-6
ATTN_SCALE = HEAD_DIM ** -0.5
MASK_VALUE = -1e30

WIN_KEYS = NA_WIN_ROWS * GRID_W
TOKEN_TILE = 512
FFN_TOKEN_TILE = 1024
HALO = 16
IN_CHUNK = 512
FF_BLOCK = 512
ROW_CHUNK = 32
NORM_UNROLL = 4
NA_ROWS_PER_STEP = 32
SUBLANES = 8
FFN_ROW_GROUPS = 2
CONV_ROWS = 16
VMEM_LIMIT = 60 * 1024 * 1024


def _rms_rows(x, g):
    ms = jnp.mean(x * x, axis=-1, keepdims=True)
    return (x * lax.rsqrt(ms + EPS)) * g


def _bias_kernel(rpb_ref, out_ref, pair_scr):
    l = pl.program_id(0)
    h = pl.program_id(1)
    shape = (GRID_W, 2 * GRID_W)
    q = lax.broadcasted_iota(jnp.int32, shape, 0)
    lane = lax.broadcasted_iota(jnp.int32, shape, 1)
    kc = lane & (GRID_W - 1)
    d = kc - q + (NA_WIN_COLS - 1)
    cs = jnp.clip(q - NA_WIN_COLS // 2, 0, GRID_W - NA_WIN_COLS)
    valid = (kc >= cs) & (kc < cs + NA_WIN_COLS)
    upper = lax.broadcasted_iota(jnp.int32, (1, 2 * GRID_W), 1) >= GRID_W
    base = (l * NA_HEADS + h) * (RPB_ROWS * RPB_COLS)
    for dr in range(RPB_ROWS - 1):
        t = jnp.full(shape, MASK_VALUE, F32)
        for dd in range(RPB_COLS):
            row = jnp.where(upper, rpb_ref[base + (dr + 1) * RPB_COLS + dd], rpb_ref[base + dr * RPB_COLS + dd])
            t = jnp.where(valid & (d == dd), jnp.broadcast_to(row, shape), t)
        pair_scr[dr] = t
    for oi in range(NA_WIN_ROWS):
        for p in range(NA_WIN_ROWS // 2):
            out_ref[oi, :, 2 * GRID_W * p:2 * GRID_W * (p + 1)] = pair_scr[2 * p - oi + NA_WIN_ROWS - 1]


def _bias_tables(na_rpb):
    depth = na_rpb.shape[0]
    return pl.pallas_call(
        _bias_kernel,
        grid=(depth, NA_HEADS),
        in_specs=[pl.BlockSpec(memory_space=pltpu.SMEM)],
        out_specs=pl.BlockSpec((None, None, NA_WIN_ROWS, GRID_W, WIN_KEYS), lambda l, h: (l, h, 0, 0, 0)),
        scratch_shapes=[pltpu.VMEM((RPB_ROWS - 1, GRID_W, 2 * GRID_W), F32)],
        out_shape=jax.ShapeDtypeStruct((depth, NA_HEADS, NA_WIN_ROWS, GRID_W, WIN_KEYS), F32),
        name="rpb_tables",
    )(na_rpb.reshape(-1))


def _mem_kv_kernel(mem_ref, g_ref, w_ref, kg_ref, k_out, v_out):
    h = _rms_rows(mem_ref[...], g_ref[...]).astype(BF16)
    kv = jnp.dot(h, w_ref[...], preferred_element_type=F32)
    for hd in range(MEM_HEADS):
        sl = slice(hd * HEAD_DIM, (hd + 1) * HEAD_DIM)
        k_out[:, sl] = _rms_rows(kv[:, sl], kg_ref[...]).astype(BF16)
    v_out[...] = kv[:, MEM_WIDTH:].astype(BF16)


def _mem_kv(mem, layer, mem_norm_g, w_mem_kv, mem_k_norm_g):
    b = mem.shape[0]
    return pl.pallas_call(
        _mem_kv_kernel,
        grid=(b,),
        in_specs=[
            pl.BlockSpec((None, N_MEM, D_MODEL), lambda i: (i, 0, 0)),
            pl.BlockSpec((None, 1, D_MODEL), lambda i: (layer, 0, 0)),
            pl.BlockSpec((None, D_MODEL, 2 * MEM_WIDTH), lambda i: (layer, 0, 0)),
            pl.BlockSpec((None, 1, HEAD_DIM), lambda i: (layer, 0, 0)),
        ],
        out_specs=[
            pl.BlockSpec((None, N_MEM, MEM_WIDTH), lambda i: (i, 0, 0)),
            pl.BlockSpec((None, N_MEM, MEM_WIDTH), lambda i: (i, 0, 0)),
        ],
        out_shape=[jax.ShapeDtypeStruct((b, N_MEM, MEM_WIDTH), BF16)] * 2,
        compiler_params=pltpu.CompilerParams(vmem_limit_bytes=VMEM_LIMIT),
        name="mem_kv",
    )(mem, mem_norm_g, w_mem_kv, mem_k_norm_g)


def _in_proj_kernel(x_ref, ng_ref, w_ref, qg_ref, kg_ref, mqg_ref, km_ref, vm_ref,
                    q_out, k_out, v_out, om_out, glu_out, h_scr, u_scr):
    tm = x_ref.shape[0]

    def norm_chunk(c, carry):
        r0 = pl.multiple_of(c * ROW_CHUNK, ROW_CHUNK)
        h_scr[pl.ds(r0, ROW_CHUNK), :] = _rms_rows(x_ref[pl.ds(r0, ROW_CHUNK), :], ng_ref[...]).astype(BF16)
        return carry

    lax.fori_loop(0, tm // ROW_CHUNK, norm_chunk, 0, unroll=NORM_UNROLL)

    heads_per_chunk = IN_CHUNK // HEAD_DIM
    for c in range(IN_WIDTH // IN_CHUNK):
        z = jnp.dot(h_scr[...], w_ref[:, c * IN_CHUNK:(c + 1) * IN_CHUNK], preferred_element_type=F32)
        col = c * IN_CHUNK
        if col < 2 * NA_WIDTH:
            is_q = col < NA_WIDTH
            out, g_ref = (q_out, qg_ref) if is_q else (k_out, kg_ref)
            head0 = (col - (0 if is_q else NA_WIDTH)) // HEAD_DIM
            for j in range(heads_per_chunk):
                zh = z[:, j * HEAD_DIM:(j + 1) * HEAD_DIM]
                out[head0 + j] = _rms_rows(zh, g_ref[...]).astype(BF16)
        elif col < 3 * NA_WIDTH:
            head0 = (col - 2 * NA_WIDTH) // HEAD_DIM
            for j in range(heads_per_chunk):
                v_out[head0 + j] = z[:, j * HEAD_DIM:(j + 1) * HEAD_DIM].astype(BF16)
        elif col < 3 * NA_WIDTH + MEM_WIDTH:
            for j in range(MEM_HEADS):
                sl = slice(j * HEAD_DIM, (j + 1) * HEAD_DIM)
                qh = _rms_rows(z[:, sl], mqg_ref[...]).astype(BF16)
                s = lax.dot_general(qh, km_ref[:, sl], (((1,), (1,)), ((), ())),
                                    preferred_element_type=F32) * ATTN_SCALE
                e = jnp.exp(s - jnp.max(s, axis=-1, keepdims=True))
                o = jnp.dot(e.astype(BF16), vm_ref[:, sl], preferred_element_type=F32)
                om_out[:, sl] = (o / jnp.sum(e, axis=-1, keepdims=True)).astype(BF16)
        elif col < 3 * NA_WIDTH + MEM_WIDTH + CONV_CH:
            u_scr[...] = z
        else:
            glu_out[...] = u_scr[...] * jax.nn.sigmoid(z)


def _in_proj(x, layer, seq_len, km, vm, norm_mix_g, w_in, na_q_norm_g, na_k_norm_g, mem_q_norm_g):
    n = x.shape[0]
    tm = TOKEN_TILE
    tiles_per_seq = seq_len // tm
    vec = lambda width: pl.BlockSpec((None, 1, width), lambda i: (layer, 0, 0))
    head_major = pl.BlockSpec((NA_HEADS, tm, HEAD_DIM), lambda i: (0, i, 0))
    mem_spec = pl.BlockSpec((None, N_MEM, MEM_WIDTH), lambda i: (i // tiles_per_seq, 0, 0))
    return pl.pallas_call(
        _in_proj_kernel,
        grid=(n // tm,),
        in_specs=[
            pl.BlockSpec((tm, D_MODEL), lambda i: (i, 0)),
            vec(D_MODEL),
            pl.BlockSpec((None, D_MODEL, IN_WIDTH), lambda i: (layer, 0, 0), pipeline_mode=pl.Buffered(1)),
            vec(HEAD_DIM), vec(HEAD_DIM), vec(HEAD_DIM),
            mem_spec, mem_spec,
        ],
        out_specs=[
            head_major, head_major, head_major,
            pl.BlockSpec((tm, MEM_WIDTH), lambda i: (i, 0)),
            pl.BlockSpec((tm, CONV_CH), lambda i: (i, 0)),
        ],
        out_shape=[
            jax.ShapeDtypeStruct((NA_HEADS, n, HEAD_DIM), BF16),
            jax.ShapeDtypeStruct((NA_HEADS, n, HEAD_DIM), BF16),
            jax.ShapeDtypeStruct((NA_HEADS, n, HEAD_DIM), BF16),
            jax.ShapeDtypeStruct((n, MEM_WIDTH), BF16),
            jax.ShapeDtypeStruct((n, CONV_CH), F32),
        ],
        scratch_shapes=[pltpu.VMEM((tm, D_MODEL), BF16), pltpu.VMEM((tm, CONV_CH), F32)],
        compiler_params=pltpu.CompilerParams(vmem_limit_bytes=VMEM_LIMIT),
        name="in_proj",
    )(x, norm_mix_g, w_in, na_q_norm_g, na_k_norm_g, mem_q_norm_g, km, vm)


def _na_kernel(q_ref, k_ref, v_ref, bias_ref, o_ref):
    rows = q_ref.shape[0] // GRID_W

    def group_step(g, carry):
        row_ids = [g * NA_ROWS_PER_STEP + u for u in range(NA_ROWS_PER_STEP)]
        starts = [jnp.clip(r - NA_WIN_ROWS // 2, 0, rows - NA_WIN_ROWS) for r in row_ids]
        q0s = [pl.multiple_of(r * GRID_W, GRID_W) for r in row_ids]
        k0s = [pl.multiple_of(rs * GRID_W, GRID_W) for rs in starts]
        scores = []
        for q0, k0 in zip(q0s, k0s):
            q = q_ref[pl.ds(q0, GRID_W), :]
            kw = k_ref[pl.ds(k0, WIN_KEYS), :]
            scores.append(lax.dot_general(q, kw, (((1,), (1,)), ((), ())), preferred_element_type=F32))
        probs, sums = [], []
        for r, rs, s in zip(row_ids, starts, scores):
            s = s * ATTN_SCALE + bias_ref[r - rs]
            e = jnp.exp(s - jnp.max(s, axis=-1, keepdims=True))
            probs.append(e.astype(BF16))
            sums.append(jnp.sum(e, axis=-1, keepdims=True))
        for q0, k0, p, l in zip(q0s, k0s, probs, sums):
            o = jnp.dot(p, v_ref[pl.ds(k0, WIN_KEYS), :], preferred_element_type=F32)
            o_ref[pl.ds(q0, GRID_W), :] = (o / l).astype(BF16)
        return carry

    lax.fori_loop(0, rows // NA_ROWS_PER_STEP, group_step, 0)


def _na_attention(q, k, v, bias, layer, seq_len):
    n = q.shape[1]
    seq = pl.BlockSpec((None, seq_len, HEAD_DIM), lambda b, h: (h, b, 0))
    return pl.pallas_call(
        _na_kernel,
        grid=(n // seq_len, NA_HEADS),
        in_specs=[seq, seq, seq,
                  pl.BlockSpec((None, None, NA_WIN_ROWS, GRID_W, WIN_KEYS), lambda b, h: (layer, h, 0, 0, 0))],
        out_specs=seq,
        out_shape=jax.ShapeDtypeStruct((NA_HEADS, n, HEAD_DIM), BF16),
        compiler_params=pltpu.CompilerParams(vmem_limit_bytes=VMEM_LIMIT),
        name="na_attn",
    )(q, k, v, bias)


def _out_proj_kernel(x_ref, ona_ref, om_ref, glu_prev, glu_cur, glu_next, cw_ref, cb_ref, lg_ref, lb_ref,
                     w_ref, y_ref, ext_scr, shift_scr, cat_scr, *, tiles_per_seq):
    tm = x_ref.shape[0]
    attn_width = NA_WIDTH + MEM_WIDTH
    pos = pl.program_id(0) % tiles_per_seq
    ext_scr[0:HALO, :] = jnp.where(pos == 0, 0.0, glu_prev[...])
    ext_scr[HALO:HALO + tm, :] = glu_cur[...]
    ext_scr[HALO + tm:, :] = jnp.where(pos == tiles_per_seq - 1, 0.0, glu_next[...])
    for hd in range(NA_HEADS):
        cat_scr[:, hd * HEAD_DIM:(hd + 1) * HEAD_DIM] = ona_ref[hd]
    cat_scr[:, NA_WIDTH:attn_width] = om_ref[...]

    for c in range(D_MODEL // IN_CHUNK):
        sl = slice(c * IN_CHUNK, (c + 1) * IN_CHUNK)
        y_ref[:, sl] = x_ref[:, sl] + jnp.dot(cat_scr[:, :attn_width], w_ref[:attn_width, sl],
                                              preferred_element_type=F32)

    lead = HALO - (CONV_K - 1) // 2
    span = shift_scr.shape[1]
    for b in range(1, SUBLANES):
        shift_scr[b - 1] = ext_scr[b:b + span, :]
    rep = lambda v: jnp.concatenate([v] * (CONV_ROWS // SUBLANES), axis=0)
    for r0 in range(0, tm, CONV_ROWS):
        acc = rep(cb_ref[...])
        for tap in range(CONV_K):
            a, b = divmod(tap + lead, SUBLANES)
            off = r0 + SUBLANES * a
            win = ext_scr[off:off + CONV_ROWS, :] if b == 0 else shift_scr[b - 1, off:off + CONV_ROWS, :]
            acc = acc + win * rep(cw_ref[tap])
        mu = jnp.mean(acc, axis=-1, keepdims=True)
        cen = acc - mu
        var = jnp.mean(cen * cen, axis=-1, keepdims=True)
        yln = (cen * lax.rsqrt(var + EPS)) * rep(lg_ref[...]) + rep(lb_ref[...])
        cat_scr[r0:r0 + CONV_ROWS, attn_width:] = (yln * jax.nn.sigmoid(yln)).astype(BF16)

    for c in range(D_MODEL // IN_CHUNK):
        sl = slice(c * IN_CHUNK, (c + 1) * IN_CHUNK)
        y_ref[:, sl] += jnp.dot(cat_scr[:, attn_width:], w_ref[attn_width:, sl], preferred_element_type=F32)


def _out_proj(x, ona, om, glu, layer, seq_len, conv_dw_w, conv_dw_b, conv_ln_g, conv_ln_b, w_out):
    n = x.shape[0]
    tm = TOKEN_TILE
    tiles_per_seq = seq_len // tm
    halo_per_tile = tm // HALO
    last_halo = n // HALO - 1
    sub_vec = pl.BlockSpec((None, SUBLANES, CONV_CH), lambda i: (layer, 0, 0))
    return pl.pallas_call(
        functools.partial(_out_proj_kernel, tiles_per_seq=tiles_per_seq),
        grid=(n // tm,),
        in_specs=[
            pl.BlockSpec((tm, D_MODEL), lambda i: (i, 0)),
            pl.BlockSpec((NA_HEADS, tm, HEAD_DIM), lambda i: (0, i, 0)),
            pl.BlockSpec((tm, MEM_WIDTH), lambda i: (i, 0)),
            pl.BlockSpec((HALO, CONV_CH), lambda i: (jnp.maximum(i * halo_per_tile - 1, 0), 0)),
            pl.BlockSpec((tm, CONV_CH), lambda i: (i, 0)),
            pl.BlockSpec((HALO, CONV_CH), lambda i: (jnp.minimum((i + 1) * halo_per_tile, last_halo), 0)),
            pl.BlockSpec((None, CONV_K, SUBLANES, CONV_CH), lambda i: (layer, 0, 0, 0)),
            sub_vec, sub_vec, sub_vec,
            pl.BlockSpec((None, D_MODEL, D_MODEL), lambda i: (layer, 0, 0), pipeline_mode=pl.Buffered(1)),
        ],
        out_specs=pl.BlockSpec((tm, D_MODEL), lambda i: (i, 0)),
        out_shape=jax.ShapeDtypeStruct((n, D_MODEL), F32),
        scratch_shapes=[
            pltpu.VMEM((tm + 2 * HALO, CONV_CH), F32),
            pltpu.VMEM((SUBLANES - 1, tm + 2 * HALO - SUBLANES, CONV_CH), F32),
            pltpu.VMEM((tm, D_MODEL), BF16),
        ],
        compiler_params=pltpu.CompilerParams(vmem_limit_bytes=VMEM_LIMIT),
        name="out_proj",
    )(x, ona, om, glu, glu, glu, conv_dw_w, conv_dw_b, conv_ln_g, conv_ln_b, w_out)


def _ffn_kernel(x_prev, x_cur, x_next, ng_ref, wg_ref, wv_ref, cw_ref, cb_ref, wd_ref, y_ref,
                h_scr, z_scr, act_scr, *, tiles_per_seq):
    tm = x_cur.shape[0]
    j = pl.program_id(1)

    @pl.when(j == 0)
    def _():
        pos = pl.program_id(0) % tiles_per_seq
        hp = _rms_rows(x_prev[...], ng_ref[...])
        h_scr[0:HALO, :] = jnp.where(pos == 0, 0.0, hp).astype(BF16)
        hn = _rms_rows(x_next[...], ng_ref[...])
        h_scr[HALO + tm:, :] = jnp.where(pos == tiles_per_seq - 1, 0.0, hn).astype(BF16)

        def norm_chunk(c, carry):
            r0 = pl.multiple_of(c * ROW_CHUNK, ROW_CHUNK)
            xc = x_cur[pl.ds(r0, ROW_CHUNK), :]
            h_scr[pl.ds(HALO + r0, ROW_CHUNK), :] = _rms_rows(xc, ng_ref[...]).astype(BF16)
            y_ref[pl.ds(r0, ROW_CHUNK), :] = xc
            return carry

        lax.fori_loop(0, tm // ROW_CHUNK, norm_chunk, 0, unroll=NORM_UNROLL)

    z_scr[:, :FF_BLOCK] = jnp.dot(h_scr[...], wg_ref[...], preferred_element_type=F32)
    z_scr[:, FF_BLOCK:] = jnp.dot(h_scr[...], wv_ref[...], preferred_element_type=F32)

    rep = lambda v: jnp.concatenate([v] * (ROW_CHUNK // SUBLANES), axis=0)

    def gated_rows(r0):
        acc = rep(cb_ref[...])
        for tap in range(3):
            off = r0 + HALO - 1 + tap
            acc = acc + z_scr[off:off + ROW_CHUNK, :] * rep(cw_ref[tap])
        gate, val = acc[:, :FF_BLOCK], acc[:, FF_BLOCK:]
        return (gate * jax.nn.sigmoid(gate)) * val

    rows = tm // FFN_ROW_GROUPS
    for lo in range(0, tm, rows):
        for r0 in range(lo, lo + rows, ROW_CHUNK):
            act_scr[r0:r0 + ROW_CHUNK, :] = gated_rows(r0).astype(BF16)
        y_ref[lo:lo + rows, :] += jnp.dot(act_scr[lo:lo + rows, :], wd_ref[...], preferred_element_type=F32)


def _ffn(x, layer, seq_len, norm_ffn_g, w_up, ffn_cw_blocks, ffn_cb_blocks, w_down):
    n = x.shape[0]
    tm = FFN_TOKEN_TILE
    tiles_per_seq = seq_len // tm
    halo_per_tile = tm // HALO
    last_halo = n // HALO - 1
    nblk = D_FF // FF_BLOCK
    return pl.pallas_call(
        functools.partial(_ffn_kernel, tiles_per_seq=tiles_per_seq),
        grid=(n // tm, nblk),
        in_specs=[
            pl.BlockSpec((HALO, D_MODEL), lambda i, j: (jnp.maximum(i * halo_per_tile - 1, 0), 0)),
            pl.BlockSpec((tm, D_MODEL), lambda i, j: (i, 0)),
            pl.BlockSpec((HALO, D_MODEL), lambda i, j: (jnp.minimum((i + 1) * halo_per_tile, last_halo), 0)),
            pl.BlockSpec((None, 1, D_MODEL), lambda i, j: (layer, 0, 0)),
            pl.BlockSpec((None, D_MODEL, FF_BLOCK), lambda i, j: (layer, 0, j)),
            pl.BlockSpec((None, D_MODEL, FF_BLOCK), lambda i, j: (layer, 0, nblk + j)),
            pl.BlockSpec((None, None, 3, SUBLANES, 2 * FF_BLOCK), lambda i, j: (layer, j, 0, 0, 0)),
            pl.BlockSpec((None, None, SUBLANES, 2 * FF_BLOCK), lambda i, j: (layer, j, 0, 0)),
            pl.BlockSpec((None, FF_BLOCK, D_MODEL), lambda i, j: (layer, j, 0)),
        ],
        out_specs=pl.BlockSpec((tm, D_MODEL), lambda i, j: (i, 0)),
        out_shape=jax.ShapeDtypeStruct((n, D_MODEL), F32),
        scratch_shapes=[
            pltpu.VMEM((tm + 2 * HALO, D_MODEL), BF16),
            pltpu.VMEM((tm + 2 * HALO, 2 * FF_BLOCK), F32),
            pltpu.VMEM((tm, FF_BLOCK), BF16),
        ],
        compiler_params=pltpu.CompilerParams(
            dimension_semantics=("arbitrary", "arbitrary"), vmem_limit_bytes=VMEM_LIMIT),
        name="ffn",
    )(x, x, x, norm_ffn_g, w_up, w_up, ffn_cw_blocks, ffn_cb_blocks, w_down)


def kernel(x_prompt, x_sample, mem_prompt, mem_sample, norm_mix_g, w_in, na_q_norm_g, na_k_norm_g, na_rpb,
           mem_norm_g, w_mem_kv, mem_q_norm_g, mem_k_norm_g, conv_dw_w, conv_dw_b, conv_ln_g, conv_ln_b, w_out,
           norm_ffn_g, w_up, ffn_dw_w, ffn_dw_b, w_down):
    depth = w_in.shape[0]
    row = lambda a: a.reshape(depth, 1, a.shape[-1])
    w_in_b, w_kv_b, w_out_b, w_up_b, w_down_b = (w.astype(BF16) for w in (w_in, w_mem_kv, w_out, w_up, w_down))
    norm_mix_g, na_q_norm_g, na_k_norm_g, mem_norm_g, mem_q_norm_g, mem_k_norm_g, norm_ffn_g = map(
        row, (norm_mix_g, na_q_norm_g, na_k_norm_g, mem_norm_g, mem_q_norm_g, mem_k_norm_g, norm_ffn_g))
    sub = lambda a: jnp.broadcast_to(a[..., None, :], a.shape[:-1] + (SUBLANES, a.shape[-1]))
    conv_dw_w, conv_dw_b, conv_ln_g, conv_ln_b = map(sub, (conv_dw_w, conv_dw_b, conv_ln_g, conv_ln_b))

    nblk = D_FF // FF_BLOCK

    def ff_blocks(a):
        lead = a.shape[:-1]
        a = a.reshape(lead + (2, nblk, FF_BLOCK))
        return jnp.moveaxis(a, -3, -2).reshape(lead + (nblk, 2 * FF_BLOCK))

    ffn_cw_blocks = sub(jnp.moveaxis(ff_blocks(ffn_dw_w), -2, 1))
    ffn_cb_blocks = sub(ff_blocks(ffn_dw_b))
    bias = _bias_tables(na_rpb)

    def trunk(x3, mem):
        b, seq_len, _ = x3.shape
        x = x3.reshape(b * seq_len, D_MODEL)
        for layer in range(depth):
            km, vm = _mem_kv(mem, layer, mem_norm_g, w_kv_b, mem_k_norm_g)
            q, k, v, om, glu = _in_proj(x, layer, seq_len, km, vm, norm_mix_g, w_in_b,
                                        na_q_norm_g, na_k_norm_g, mem_q_norm_g)
            ona = _na_attention(q, k, v, bias, layer, seq_len)
            x = _out_proj(x, ona, om, glu, layer, seq_len, conv_dw_w, conv_dw_b, conv_ln_g, conv_ln_b, w_out_b)
            x = _ffn(x, layer, seq_len, norm_ffn_g, w_up_b, ffn_cw_blocks, ffn_cb_blocks, w_down_b)
        return x.reshape(b, seq_len, D_MODEL)

    return trunk(x_prompt, mem_prompt), trunk(x_sample, mem_sample)
```

```python
import functools

import jax
import jax.numpy as jnp
from jax import lax
from jax.experimental import pallas as pl
from jax.experimental.pallas import tpu as pltpu

F32 = jnp.float32
BF16 = jnp.bfloat16

D_MODEL = 2048
GRID_W = 64
HEAD_DIM = 128
NA_HEADS = 8
NA_WIDTH = NA_HEADS * HEAD_DIM
NA_WIN_ROWS = 8
NA_WIN_COLS = 16
RPB_ROWS = 2 * NA_WIN_ROWS - 1
RPB_COLS = 2 * NA_WIN_COLS - 1
MEM_HEADS = 4
MEM_WIDTH = MEM_HEADS * HEAD_DIM
N_MEM = 256
CONV_CH = D_MODEL - NA_WIDTH - MEM_WIDTH
CONV_K = 31
D_FF = 5632
IN_WIDTH = 3 * NA_WIDTH + MEM_WIDTH + 2 * CONV_CH
EPS = 1e-6
ATTN_SCALE = HEAD_DIM ** -0.5
MASK_VALUE = -1e30

WIN_KEYS = NA_WIN_ROWS * GRID_W
TOKEN_TILE = 512
FFN_TOKEN_TILE = 1024
HALO = 16
IN_CHUNK = 512
FF_BLOCK = 512
ROW_CHUNK = 32
NORM_UNROLL = 4
NA_ROWS_PER_STEP = 64
SUBLANES = 8
FFN_ROW_GROUPS = 2
CONV_ROWS = 16
VMEM_LIMIT = 60 * 1024 * 1024


def _rms_rows(x, g):
    ms = jnp.mean(x * x, axis=-1, keepdims=True)
    return (x * lax.rsqrt(ms + EPS)) * g


def _bias_kernel(rpb_ref, out_ref, pair_scr):
    l = pl.program_id(0)
    h = pl.program_id(1)
    shape = (GRID_W, 2 * GRID_W)
    q = lax.broadcasted_iota(jnp.int32, shape, 0)
    lane = lax.broadcasted_iota(jnp.int32, shape, 1)
    kc = lane & (GRID_W - 1)
    d = kc - q + (NA_WIN_COLS - 1)
    cs = jnp.clip(q - NA_WIN_COLS // 2, 0, GRID_W - NA_WIN_COLS)
    valid = (kc >= cs) & (kc < cs + NA_WIN_COLS)
    upper = lax.broadcasted_iota(jnp.int32, (1, 2 * GRID_W), 1) >= GRID_W
    base = (l * NA_HEADS + h) * (RPB_ROWS * RPB_COLS)
    for dr in range(RPB_ROWS - 1):
        t = jnp.full(shape, MASK_VALUE, F32)
        for dd in range(RPB_COLS):
            row = jnp.where(upper, rpb_ref[base + (dr + 1) * RPB_COLS + dd], rpb_ref[base + dr * RPB_COLS + dd])
            t = jnp.where(valid & (d == dd), jnp.broadcast_to(row, shape), t)
        pair_scr[dr] = t
    for oi in range(NA_WIN_ROWS):
        for p in range(NA_WIN_ROWS // 2):
            out_ref[oi, :, 2 * GRID_W * p:2 * GRID_W * (p + 1)] = pair_scr[2 * p - oi + NA_WIN_ROWS - 1]


def _bias_tables(na_rpb):
    depth = na_rpb.shape[0]
    return pl.pallas_call(
        _bias_kernel,
        grid=(depth, NA_HEADS),
        in_specs=[pl.BlockSpec(memory_space=pltpu.SMEM)],
        out_specs=pl.BlockSpec((None, None, NA_WIN_ROWS, GRID_W, WIN_KEYS), lambda l, h: (l, h, 0, 0, 0)),
        scratch_shapes=[pltpu.VMEM((RPB_ROWS - 1, GRID_W, 2 * GRID_W), F32)],
        out_shape=jax.ShapeDtypeStruct((depth, NA_HEADS, NA_WIN_ROWS, GRID_W, WIN_KEYS), F32),
        name="rpb_tables",
    )(na_rpb.reshape(-1))


def _mem_kv_kernel(mem_ref, g_ref, w_ref, kg_ref, k_out, v_out):
    h = _rms_rows(mem_ref[...], g_ref[...]).astype(BF16)
    kv = jnp.dot(h, w_ref[...], preferred_element_type=F32)
    for hd in range(MEM_HEADS):
        sl = slice(hd * HEAD_DIM, (hd + 1) * HEAD_DIM)
        k_out[:, sl] = _rms_rows(kv[:, sl], kg_ref[...]).astype(BF16)
    v_out[...] = kv[:, MEM_WIDTH:].astype(BF16)


def _mem_kv(mem, layer, mem_norm_g, w_mem_kv, mem_k_norm_g):
    b = mem.shape[0]
    return pl.pallas_call(
        _mem_kv_kernel,
        grid=(b,),
        in_specs=[
            pl.BlockSpec((None, N_MEM, D_MODEL), lambda i: (i, 0, 0)),
            pl.BlockSpec((None, 1, D_MODEL), lambda i: (layer, 0, 0)),
            pl.BlockSpec((None, D_MODEL, 2 * MEM_WIDTH), lambda i: (layer, 0, 0)),
            pl.BlockSpec((None, 1, HEAD_DIM), lambda i: (layer, 0, 0)),
        ],
        out_specs=[
            pl.BlockSpec((None, N_MEM, MEM_WIDTH), lambda i: (i, 0, 0)),
            pl.BlockSpec((None, N_MEM, MEM_WIDTH), lambda i: (i, 0, 0)),
        ],
        out_shape=[jax.ShapeDtypeStruct((b, N_MEM, MEM_WIDTH), BF16)] * 2,
        compiler_params=pltpu.CompilerParams(vmem_limit_bytes=VMEM_LIMIT),
        name="mem_kv",
    )(mem, mem_norm_g, w_mem_kv, mem_k_norm_g)


def _in_proj_kernel(x_ref, ng_ref, w_ref, qg_ref, kg_ref, mqg_ref, km_ref, vm_ref,
                    q_out, k_out, v_out, om_out, glu_out, h_scr, u_scr):
    tm = x_ref.shape[0]

    def norm_chunk(c, carry):
        r0 = pl.multiple_of(c * ROW_CHUNK, ROW_CHUNK)
        h_scr[pl.ds(r0, ROW_CHUNK), :] = _rms_rows(x_ref[pl.ds(r0, ROW_CHUNK), :], ng_ref[...]).astype(BF16)
        return carry

    lax.fori_loop(0, tm // ROW_CHUNK, norm_chunk, 0, unroll=NORM_UNROLL)

    heads_per_chunk = IN_CHUNK // HEAD_DIM
    for c in range(IN_WIDTH // IN_CHUNK):
        z = jnp.dot(h_scr[...], w_ref[:, c * IN_CHUNK:(c + 1) * IN_CHUNK], preferred_element_type=F32)
        col = c * IN_CHUNK
        if col < 2 * NA_WIDTH:
            is_q = col < NA_WIDTH
            out, g_ref = (q_out, qg_ref) if is_q else (k_out, kg_ref)
            head0 = (col - (0 if is_q else NA_WIDTH)) // HEAD_DIM
            for j in range(heads_per_chunk):
                zh = z[:, j * HEAD_DIM:(j + 1) * HEAD_DIM]
                out[head0 + j] = _rms_rows(zh, g_ref[...]).astype(BF16)
        elif col < 3 * NA_WIDTH:
            head0 = (col - 2 * NA_WIDTH) // HEAD_DIM
            for j in range(heads_per_chunk):
                v_out[head0 + j] = z[:, j * HEAD_DIM:(j + 1) * HEAD_DIM].astype(BF16)
        elif col < 3 * NA_WIDTH + MEM_WIDTH:
            for j in range(MEM_HEADS):
                sl = slice(j * HEAD_DIM, (j + 1) * HEAD_DIM)
                qh = _rms_rows(z[:, sl], mqg_ref[...]).astype(BF16)
                s = lax.dot_general(qh, km_ref[:, sl], (((1,), (1,)), ((), ())),
                                    preferred_element_type=F32) * ATTN_SCALE
                e = jnp.exp(s - jnp.max(s, axis=-1, keepdims=True))
                o = jnp.dot(e.astype(BF16), vm_ref[:, sl], preferred_element_type=F32)
                om_out[:, sl] = (o / jnp.sum(e, axis=-1, keepdims=True)).astype(BF16)
        elif col < 3 * NA_WIDTH + MEM_WIDTH + CONV_CH:
            u_scr[...] = z
        else:
            glu_out[...] = u_scr[...] * jax.nn.sigmoid(z)


def _in_proj(x, layer, seq_len, km, vm, norm_mix_g, w_in, na_q_norm_g, na_k_norm_g, mem_q_norm_g):
    n = x.shape[0]
    tm = TOKEN_TILE
    tiles_per_seq = seq_len // tm
    vec = lambda width: pl.BlockSpec((None, 1, width), lambda i: (layer, 0, 0))
    head_major = pl.BlockSpec((NA_HEADS, tm, HEAD_DIM), lambda i: (0, i, 0))
    mem_spec = pl.BlockSpec((None, N_MEM, MEM_WIDTH), lambda i: (i // tiles_per_seq, 0, 0))
    return pl.pallas_call(
        _in_proj_kernel,
        grid=(n // tm,),
        in_specs=[
            pl.BlockSpec((tm, D_MODEL), lambda i: (i, 0)),
            vec(D_MODEL),
            pl.BlockSpec((None, D_MODEL, IN_WIDTH), lambda i: (layer, 0, 0), pipeline_mode=pl.Buffered(1)),
            vec(HEAD_DIM), vec(HEAD_DIM), vec(HEAD_DIM),
            mem_spec, mem_spec,
        ],
        out_specs=[
            head_major, head_major, head_major,
            pl.BlockSpec((tm, MEM_WIDTH), lambda i: (i, 0)),
            pl.BlockSpec((tm, CONV_CH), lambda i: (i, 0)),
        ],
        out_shape=[
            jax.ShapeDtypeStruct((NA_HEADS, n, HEAD_DIM), BF16),
            jax.ShapeDtypeStruct((NA_HEADS, n, HEAD_DIM), BF16),
            jax.ShapeDtypeStruct((NA_HEADS, n, HEAD_DIM), BF16),
            jax.ShapeDtypeStruct((n, MEM_WIDTH), BF16),
            jax.ShapeDtypeStruct((n, CONV_CH), F32),
        ],
        scratch_shapes=[pltpu.VMEM((tm, D_MODEL), BF16), pltpu.VMEM((tm, CONV_CH), F32)],
        compiler_params=pltpu.CompilerParams(vmem_limit_bytes=VMEM_LIMIT),
        name="in_proj",
    )(x, norm_mix_g, w_in, na_q_norm_g, na_k_norm_g, mem_q_norm_g, km, vm)


def _na_kernel(q_ref, k_ref, v_ref, bias_ref, o_ref):
    rows = q_ref.shape[0] // GRID_W

    def group_step(g, carry):
        row_ids = [g * NA_ROWS_PER_STEP + u for u in range(NA_ROWS_PER_STEP)]
        starts = [jnp.clip(r - NA_WIN_ROWS // 2, 0, rows - NA_WIN_ROWS) for r in row_ids]
        q0s = [pl.multiple_of(r * GRID_W, GRID_W) for r in row_ids]
        k0s = [pl.multiple_of(rs * GRID_W, GRID_W) for rs in starts]
        scores = []
        for q0, k0 in zip(q0s, k0s):
            q = q_ref[pl.ds(q0, GRID_W), :]
            kw = k_ref[pl.ds(k0, WIN_KEYS), :]
            scores.append(lax.dot_general(q, kw, (((1,), (1,)), ((), ())), preferred_element_type=F32))
        probs, sums = [], []
        for r, rs, s in zip(row_ids, starts, scores):
            s = s * ATTN_SCALE + bias_ref[r - rs]
            e = jnp.exp(s - jnp.max(s, axis=-1, keepdims=True))
            probs.append(e.astype(BF16))
            sums.append(jnp.sum(e, axis=-1, keepdims=True))
        for q0, k0, p, l in zip(q0s, k0s, probs, sums):
            o = jnp.dot(p, v_ref[pl.ds(k0, WIN_KEYS), :], preferred_element_type=F32)
            o_ref[pl.ds(q0, GRID_W), :] = (o / l).astype(BF16)
        return carry

    lax.fori_loop(0, rows // NA_ROWS_PER_STEP, group_step, 0)


def _na_attention(q, k, v, bias, layer, seq_len):
    n = q.shape[1]
    seq = pl.BlockSpec((None, seq_len, HEAD_DIM), lambda b, h: (h, b, 0))
    return pl.pallas_call(
        _na_kernel,
        grid=(n // seq_len, NA_HEADS),
        in_specs=[seq, seq, seq,
                  pl.BlockSpec((None, None, NA_WIN_ROWS, GRID_W, WIN_KEYS), lambda b, h: (layer, h, 0, 0, 0))],
        out_specs=seq,
        out_shape=jax.ShapeDtypeStruct((NA_HEADS, n, HEAD_DIM), BF16),
        compiler_params=pltpu.CompilerParams(vmem_limit_bytes=VMEM_LIMIT),
        name="na_attn",
    )(q, k, v, bias)


def _out_proj_kernel(x_ref, ona_ref, om_ref, glu_prev, glu_cur, glu_next, cw_ref, cb_ref, lg_ref, lb_ref,
                     w_ref, y_ref, ext_scr, shift_scr, cat_scr, *, tiles_per_seq):
    tm = x_ref.shape[0]
    attn_width = NA_WIDTH + MEM_WIDTH
    pos = pl.program_id(0) % tiles_per_seq
    ext_scr[0:HALO, :] = jnp.where(pos == 0, 0.0, glu_prev[...])
    ext_scr[HALO:HALO + tm, :] = glu_cur[...]
    ext_scr[HALO + tm:, :] = jnp.where(pos == tiles_per_seq - 1, 0.0, glu_next[...])
    for hd in range(NA_HEADS):
        cat_scr[:, hd * HEAD_DIM:(hd + 1) * HEAD_DIM] = ona_ref[hd]
    cat_scr[:, NA_WIDTH:attn_width] = om_ref[...]

    for c in range(D_MODEL // IN_CHUNK):
        sl = slice(c * IN_CHUNK, (c + 1) * IN_CHUNK)
        y_ref[:, sl] = x_ref[:, sl] + jnp.dot(cat_scr[:, :attn_width], w_ref[:attn_width, sl],
                                              preferred_element_type=F32)

    lead = HALO - (CONV_K - 1) // 2
    span = shift_scr.shape[1]
    for b in range(1, SUBLANES):
        shift_scr[b - 1] = ext_scr[b:b + span, :]
    rep = lambda v: jnp.concatenate([v] * (CONV_ROWS // SUBLANES), axis=0)
    for r0 in range(0, tm, CONV_ROWS):
        acc = rep(cb_ref[...])
        for tap in range(CONV_K):
            a, b = divmod(tap + lead, SUBLANES)
            off = r0 + SUBLANES * a
            win = ext_scr[off:off + CONV_ROWS, :] if b == 0 else shift_scr[b - 1, off:off + CONV_ROWS, :]
            acc = acc + win * rep(cw_ref[tap])
        mu = jnp.mean(acc, axis=-1, keepdims=True)
        cen = acc - mu
        var = jnp.mean(cen * cen, axis=-1, keepdims=True)
        yln = (cen * lax.rsqrt(var + EPS)) * rep(lg_ref[...]) + rep(lb_ref[...])
        cat_scr[r0:r0 + CONV_ROWS, attn_width:] = (yln * jax.nn.sigmoid(yln)).astype(BF16)

    for c in range(D_MODEL // IN_CHUNK):
        sl = slice(c * IN_CHUNK, (c + 1) * IN_CHUNK)
        y_ref[:, sl] += jnp.dot(cat_scr[:, attn_width:], w_ref[attn_width:, sl], preferred_element_type=F32)


def _out_proj(x, ona, om, glu, layer, seq_len, conv_dw_w, conv_dw_b, conv_ln_g, conv_ln_b, w_out):
    n = x.shape[0]
    tm = TOKEN_TILE
    tiles_per_seq = seq_len // tm
    halo_per_tile = tm // HALO
    last_halo = n // HALO - 1
    sub_vec = pl.BlockSpec((None, SUBLANES, CONV_CH), lambda i: (layer, 0, 0))
    return pl.pallas_call(
        functools.partial(_out_proj_kernel, tiles_per_seq=tiles_per_seq),
        grid=(n // tm,),
        in_specs=[
            pl.BlockSpec((tm, D_MODEL), lambda i: (i, 0)),
            pl.BlockSpec((NA_HEADS, tm, HEAD_DIM), lambda i: (0, i, 0)),
            pl.BlockSpec((tm, MEM_WIDTH), lambda i: (i, 0)),
            pl.BlockSpec((HALO, CONV_CH), lambda i: (jnp.maximum(i * halo_per_tile - 1, 0), 0)),
            pl.BlockSpec((tm, CONV_CH), lambda i: (i, 0)),
            pl.BlockSpec((HALO, CONV_CH), lambda i: (jnp.minimum((i + 1) * halo_per_tile, last_halo), 0)),
            pl.BlockSpec((None, CONV_K, SUBLANES, CONV_CH), lambda i: (layer, 0, 0, 0)),
            sub_vec, sub_vec, sub_vec,
            pl.BlockSpec((None, D_MODEL, D_MODEL), lambda i: (layer, 0, 0), pipeline_mode=pl.Buffered(1)),
        ],
        out_specs=pl.BlockSpec((tm, D_MODEL), lambda i: (i, 0)),
        out_shape=jax.ShapeDtypeStruct((n, D_MODEL), F32),
        scratch_shapes=[
            pltpu.VMEM((tm + 2 * HALO, CONV_CH), F32),
            pltpu.VMEM((SUBLANES - 1, tm + 2 * HALO - SUBLANES, CONV_CH), F32),
            pltpu.VMEM((tm, D_MODEL), BF16),
        ],
        compiler_params=pltpu.CompilerParams(vmem_limit_bytes=VMEM_LIMIT),
        name="out_proj",
    )(x, ona, om, glu, glu, glu, conv_dw_w, conv_dw_b, conv_ln_g, conv_ln_b, w_out)


def _ffn_kernel(x_prev, x_cur, x_next, ng_ref, wg_ref, wv_ref, cw_ref, cb_ref, wd_ref, y_ref,
                h_scr, z_scr, act_scr, *, tiles_per_seq):
    tm = x_cur.shape[0]
    j = pl.program_id(1)

    @pl.when(j == 0)
    def _():
        pos = pl.program_id(0) % tiles_per_seq
        hp = _rms_rows(x_prev[...], ng_ref[...])
        h_scr[0:HALO, :] = jnp.where(pos == 0, 0.0, hp).astype(BF16)
        hn = _rms_rows(x_next[...], ng_ref[...])
        h_scr[HALO + tm:, :] = jnp.where(pos == tiles_per_seq - 1, 0.0, hn).astype(BF16)

        def norm_chunk(c, carry):
            r0 = pl.multiple_of(c * ROW_CHUNK, ROW_CHUNK)
            xc = x_cur[pl.ds(r0, ROW_CHUNK), :]
            h_scr[pl.ds(HALO + r0, ROW_CHUNK), :] = _rms_rows(xc, ng_ref[...]).astype(BF16)
            y_ref[pl.ds(r0, ROW_CHUNK), :] = xc
            return carry

        lax.fori_loop(0, tm // ROW_CHUNK, norm_chunk, 0, unroll=NORM_UNROLL)

    z_scr[:, :FF_BLOCK] = jnp.dot(h_scr[...], wg_ref[...], preferred_element_type=F32)
    z_scr[:, FF_BLOCK:] = jnp.dot(h_scr[...], wv_ref[...], preferred_element_type=F32)

    rep = lambda v: jnp.concatenate([v] * (ROW_CHUNK // SUBLANES), axis=0)

    def gated_rows(r0):
        acc = rep(cb_ref[...])
        for tap in range(3):
            off = r0 + HALO - 1 + tap
            acc = acc + z_scr[off:off + ROW_CHUNK, :] * rep(cw_ref[tap])
        gate, val = acc[:, :FF_BLOCK], acc[:, FF_BLOCK:]
        return (gate * jax.nn.sigmoid(gate)) * val

    rows = tm // FFN_ROW_GROUPS
    for lo in range(0, tm, rows):
        for r0 in range(lo, lo + rows, ROW_CHUNK):
            act_scr[r0:r0 + ROW_CHUNK, :] = gated_rows(r0).astype(BF16)
        y_ref[lo:lo + rows, :] += jnp.dot(act_scr[lo:lo + rows, :], wd_ref[...], preferred_element_type=F32)


def _ffn(x, layer, seq_len, norm_ffn_g, w_up, ffn_cw_blocks, ffn_cb_blocks, w_down):
    n = x.shape[0]
    tm = FFN_TOKEN_TILE
    tiles_per_seq = seq_len // tm
    halo_per_tile = tm // HALO
    last_halo = n // HALO - 1
    nblk = D_FF // FF_BLOCK
    return pl.pallas_call(
        functools.partial(_ffn_kernel, tiles_per_seq=tiles_per_seq),
        grid=(n // tm, nblk),
        in_specs=[
            pl.BlockSpec((HALO, D_MODEL), lambda i, j: (jnp.maximum(i * halo_per_tile - 1, 0), 0)),
            pl.BlockSpec((tm, D_MODEL), lambda i, j: (i, 0)),
            pl.BlockSpec((HALO, D_MODEL), lambda i, j: (jnp.minimum((i + 1) * halo_per_tile, last_halo), 0)),
            pl.BlockSpec((None, 1, D_MODEL), lambda i, j: (layer, 0, 0)),
            pl.BlockSpec((None, D_MODEL, FF_BLOCK), lambda i, j: (layer, 0, j)),
            pl.BlockSpec((None, D_MODEL, FF_BLOCK), lambda i, j: (layer, 0, nblk + j)),
            pl.BlockSpec((None, None, 3, SUBLANES, 2 * FF_BLOCK), lambda i, j: (layer, j, 0, 0, 0)),
            pl.BlockSpec((None, None, SUBLANES, 2 * FF_BLOCK), lambda i, j: (layer, j, 0, 0)),
            pl.BlockSpec((None, FF_BLOCK, D_MODEL), lambda i, j: (layer, j, 0)),
        ],
        out_specs=pl.BlockSpec((tm, D_MODEL), lambda i, j: (i, 0)),
        out_shape=jax.ShapeDtypeStruct((n, D_MODEL), F32),
        scratch_shapes=[
            pltpu.VMEM((tm + 2 * HALO, D_MODEL), BF16),
            pltpu.VMEM((tm + 2 * HALO, 2 * FF_BLOCK), F32),
            pltpu.VMEM((tm, FF_BLOCK), BF16),
        ],
        compiler_params=pltpu.CompilerParams(
            dimension_semantics=("arbitrary", "arbitrary"), vmem_limit_bytes=VMEM_LIMIT),
        name="ffn",
    )(x, x, x, norm_ffn_g, w_up, w_up, ffn_cw_blocks, ffn_cb_blocks, w_down)


def kernel(x_prompt, x_sample, mem_prompt, mem_sample, norm_mix_g, w_in, na_q_norm_g, na_k_norm_g, na_rpb,
           mem_norm_g, w_mem_kv, mem_q_norm_g, mem_k_norm_g, conv_dw_w, conv_dw_b, conv_ln_g, conv_ln_b, w_out,
           norm_ffn_g, w_up, ffn_dw_w, ffn_dw_b, w_down):
    depth = w_in.shape[0]
    row = lambda a: a.reshape(depth, 1, a.shape[-1])
    w_in_b, w_kv_b, w_out_b, w_up_b, w_down_b = (w.astype(BF16) for w in (w_in, w_mem_kv, w_out, w_up, w_down))
    norm_mix_g, na_q_norm_g, na_k_norm_g, mem_norm_g, mem_q_norm_g, mem_k_norm_g, norm_ffn_g = map(
        row, (norm_mix_g, na_q_norm_g, na_k_norm_g, mem_norm_g, mem_q_norm_g, mem_k_norm_g, norm_ffn_g))
    sub = lambda a: jnp.broadcast_to(a[..., None, :], a.shape[:-1] + (SUBLANES, a.shape[-1]))
    conv_dw_w, conv_dw_b, conv_ln_g, conv_ln_b = map(sub, (conv_dw_w, conv_dw_b, conv_ln_g, conv_ln_b))

    nblk = D_FF // FF_BLOCK

    def ff_blocks(a):
        lead = a.shape[:-1]
        a = a.reshape(lead + (2, nblk, FF_BLOCK))
        return jnp.moveaxis(a, -3, -2).reshape(lead + (nblk, 2 * FF_BLOCK))

    ffn_cw_blocks = sub(jnp.moveaxis(ff_blocks(ffn_dw_w), -2, 1))
    ffn_cb_blocks = sub(ff_blocks(ffn_dw_b))
    bias = _bias_tables(na_rpb)

    def trunk(x3, mem):
        b, seq_len, _ = x3.shape
        x = x3.reshape(b * seq_len, D_MODEL)
        for layer in range(depth):
            km, vm = _mem_kv(mem, layer, mem_norm_g, w_kv_b, mem_k_norm_g)
            q, k, v, om, glu = _in_proj(x, layer, seq_len, km, vm, norm_mix_g, w_in_b,
                                        na_q_norm_g, na_k_norm_g, mem_q_norm_g)
            ona = _na_attention(q, k, v, bias, layer, seq_len)
            x = _out_proj(x, ona, om, glu, layer, seq_len, conv_dw_w, conv_dw_b, conv_ln_g, conv_ln_b, w_out_b)
            x = _ffn(x, layer, seq_len, norm_ffn_g, w_up_b, ffn_cw_blocks, ffn_cb_blocks, w_down_b)
        return x.reshape(b, seq_len, D_MODEL)

    return trunk(x_prompt, mem_prompt), trunk(x_sample, mem_sample)
```
